```python
import math
import jax, jax.numpy as jnp
from jax import lax
import numpy as np

D_MODEL = 1024
BATCH = 8
SEQ = 2048
DEPTH = 2
DEC_BATCH = 32
DEC_SEQ = 8
PAST_LEN = 8192
PAGE_SIZE = 128

N_AB = (DEPTH + 1) // 2
N_C = DEPTH // 2
POOL_WINDOWS = (2, 4, 8, 16)
POOL_GROUPS = len(POOL_WINDOWS)
D_A = D_MODEL // 4
POOL_GC = D_A // POOL_GROUPS
POOL_BUF = max(POOL_WINDOWS) - 1
SB_HEAD_DIM = 64
D_B = D_MODEL - D_A
SB_HEADS = D_B // SB_HEAD_DIM
Q_BLOCK = 128
SB_BIAS_LO = -9.0
SB_BIAS_HI = -6.0
D_AB_IN = D_A + 3 * D_B
D_INNER = 2 * D_MODEL
SSD_HEAD_DIM = 64
SSD_HEADS = D_INNER // SSD_HEAD_DIM
SSD_GROUPS = 4
SSD_STATE = 128
SSD_CONV = 4
SSD_CHUNK = 128
CONV_DIM = D_INNER + 2 * SSD_GROUPS * SSD_STATE
D_SSD_IN = D_INNER + CONV_DIM + SSD_HEADS
PEER_HEADS = 8
N_KEYS = 128
N_EXPERTS = N_KEYS * N_KEYS
PEER_TOPK = 16
D_KEY = 256
D_HALF = D_KEY // 2
TOK_BLOCK = 256
EPS = 1e-6

kernel_name = 'hybrid_pool_stickbreak_ssd_peer_decode_step'


def rmsnorm(x, g):
    xf = x.astype(jnp.float32)
    r = lax.rsqrt(jnp.mean(xf * xf, axis=-1, keepdims=True) + EPS)
    return (xf * r).astype(x.dtype) * g


def modulate(x, g, shift, scale):
    return rmsnorm(x, g) * (1 + scale[:, None, :]) + shift[:, None, :]


def pool_mixer(a, buf, start_pos, pool_w, pool_scale):
    bsz, t, _ = a.shape
    raw = jnp.concatenate([buf.astype(a.dtype), a], axis=1)
    full = raw.astype(jnp.float32)
    cs = jnp.concatenate([jnp.zeros_like(full[:, :1]), jnp.cumsum(full, axis=1)], axis=1)
    pos = start_pos + jnp.arange(t)
    hi = cs[:, POOL_BUF + 1:POOL_BUF + 1 + t]
    means = []
    for g, w in enumerate(POOL_WINDOWS):
        sl = slice(g * POOL_GC, (g + 1) * POOL_GC)
        lo = cs[:, POOL_BUF + 1 - w:POOL_BUF + 1 - w + t, sl]
        cnt = jnp.minimum(pos + 1, w).astype(jnp.float32)[None, :, None]
        means.append((hi[..., sl] - lo) / cnt)
    mean = jnp.stack(means, axis=2)
    diff = (mean - full[:, POOL_BUF:].reshape(bsz, t, POOL_GROUPS, POOL_GC)).astype(a.dtype)
    y = jnp.einsum('btgc,gcd->btgd', diff, pool_w).reshape(bsz, t, D_A) * pool_scale
    return y, raw[:, -POOL_BUF:]


def stick_breaking(q, k, v, q_pos, k_pos, bias):
    bsz, tq, h, dh = q.shape
    qb = min(Q_BLOCK, tq)
    nb = -(-tq // qb)
    pad = nb * qb - tq
    qp = jnp.pad(q, ((0, 0), (0, pad), (0, 0), (0, 0))).reshape(bsz, nb, qb, h, dh)
    qp = jnp.moveaxis(qp, 1, 0)
    pp = jnp.pad(q_pos, (0, pad), constant_values=-1).reshape(nb, qb)
    scale = dh ** -0.5
    bias_f = bias.astype(jnp.float32)[None, :, None, None]

    def block(args):
        qblk, pblk = args
        z = jnp.einsum('bqhd,bkhd->bhqk', qblk, k, preferred_element_type=jnp.float32) * scale + bias_f
        valid = k_pos[None, :] < pblk[:, None]
        log_keep = jnp.where(valid, jax.nn.log_sigmoid(-z), 0.0)
        after = lax.cumsum(log_keep, axis=3, reverse=True) - log_keep
        wts = jnp.where(valid, jnp.exp(jax.nn.log_sigmoid(z) + after), 0.0)
        return jnp.einsum('bhqk,bkhd->bqhd', wts.astype(v.dtype), v)

    out = lax.map(block, (qp, pp))
    return jnp.moveaxis(out, 0, 1).reshape(bsz, nb * qb, h, dh)[:, :tq]


def mixer_ab(h, start_pos, pool_buf, past_k, past_v, w_in, w_out, pool_w, pool_scale, sb_bias):
    bsz, t, _ = h.shape
    proj = h @ w_in
    a = proj[..., :D_A]
    q, k, v = jnp.split(proj[..., D_A:], 3, axis=-1)
    q = q.reshape(bsz, t, SB_HEADS, SB_HEAD_DIM)
    k = k.reshape(bsz, t, SB_HEADS, SB_HEAD_DIM)
    v = v.reshape(bsz, t, SB_HEADS, SB_HEAD_DIM)
    ya, new_buf = pool_mixer(a, pool_buf, start_pos, pool_w, pool_scale)
    q_pos = start_pos + jnp.arange(t)
    if past_k is None:
        k_all, v_all, k_pos = k, v, q_pos
    else:
        k_all = jnp.concatenate([past_k.astype(k.dtype), k], axis=1)
        v_all = jnp.concatenate([past_v.astype(v.dtype), v], axis=1)
        k_pos = jnp.arange(past_k.shape[1] + t)
    yb = stick_breaking(q, k_all, v_all, q_pos, k_pos, sb_bias).reshape(bsz, t, D_B)
    y = jnp.concatenate([ya, yb], axis=-1) @ w_out
    return y, k, v, new_buf


def causal_conv(u, buf, w, b):
    t = u.shape[1]
    full = jnp.concatenate([buf.astype(u.dtype), u], axis=1)
    y = b + sum(full[:, i:i + t] * w[i] for i in range(SSD_CONV))
    return y, full[:, -(SSD_CONV - 1):]


def ssd_scan(x, dt, a, b_in, c_in, h0):
    bsz, t = x.shape[0], x.shape[1]
    cl = min(SSD_CHUNK, t)
    nc = -(-t // cl)
    pad = nc * cl - t
    r = SSD_HEADS // SSD_GROUPS

    def padt(u):
        return jnp.pad(u.astype(jnp.float32), ((0, 0), (0, pad)) + ((0, 0),) * (u.ndim - 2))

    xc = padt(x).reshape(bsz, nc, cl, SSD_GROUPS, r, SSD_HEAD_DIM)
    dtc = padt(dt).reshape(bsz, nc, cl, SSD_GROUPS, r)
    bc = padt(b_in).reshape(bsz, nc, cl, SSD_GROUPS, SSD_STATE)
    cc = padt(c_in).reshape(bsz, nc, cl, SSD_GROUPS, SSD_STATE)
    acs = jnp.cumsum(dtc * a.astype(jnp.float32).reshape(SSD_GROUPS, r), axis=2)
    xdt = xc * dtc[..., None]
    causal = jnp.tril(jnp.ones((cl, cl), dtype=bool))
    seg = acs[:, :, :, None] - acs[:, :, None, :]
    decay = jnp.exp(jnp.where(causal[:, :, None, None], seg, -jnp.inf))
    cb = jnp.einsum('bclgn,bcsgn->bclsg', cc, bc)
    y_diag = jnp.einsum('bclsgr,bcsgrp->bclgrp', cb[..., None] * decay, xdt)
    to_end = jnp.exp(acs[:, :, -1:] - acs)
    chunk_states = jnp.einsum('bclgn,bclgr,bclgrp->bcgrpn', bc, to_end, xdt)
    chunk_decay = jnp.exp(acs[:, :, -1])

    def step(hc, inp):
        st, dec = inp
        return hc * dec[..., None, None] + st, hc

    h_last, h_in = lax.scan(step, h0.reshape(bsz, SSD_GROUPS, r, SSD_HEAD_DIM, SSD_STATE),
                            (jnp.moveaxis(chunk_states, 1, 0), jnp.moveaxis(chunk_decay, 1, 0)))
    h_in = jnp.moveaxis(h_in, 0, 1)
    y_off = jnp.einsum('bclgn,bcgrpn,bclgr->bclgrp', cc, h_in, jnp.exp(acs))
    y = (y_diag + y_off).reshape(bsz, nc * cl, SSD_HEADS, SSD_HEAD_DIM)[:, :t]
    return y, h_last.reshape(bsz, SSD_HEADS, SSD_HEAD_DIM, SSD_STATE)


def mixer_ssd(h, conv_buf, ssm_state, w_in, conv_w, conv_b, dt_bias, a_log, d_skip, norm_g, w_out):
    bsz, t, _ = h.shape
    proj = h @ w_in
    z = proj[..., :D_INNER]
    xbc = proj[..., D_INNER:D_INNER + CONV_DIM]
    dt_raw = proj[..., D_INNER + CONV_DIM:]
    xbc, new_conv = causal_conv(xbc, conv_buf, conv_w, conv_b)
    xbc = jax.nn.silu(xbc)
    xs = xbc[..., :D_INNER].reshape(bsz, t, SSD_HEADS, SSD_HEAD_DIM)
    b_in = xbc[..., D_INNER:D_INNER + SSD_GROUPS * SSD_STATE].reshape(bsz, t, SSD_GROUPS, SSD_STATE)
    c_in = xbc[..., D_INNER + SSD_GROUPS * SSD_STATE:].reshape(bsz, t, SSD_GROUPS, SSD_STATE)
    dt = jax.nn.softplus(dt_raw.astype(jnp.float32) + dt_bias.astype(jnp.float32))
    a = -jnp.exp(a_log.astype(jnp.float32))
    y, h_new = ssd_scan(xs, dt, a, b_in, c_in, ssm_state.astype(jnp.float32))
    y = y + xs.astype(jnp.float32) * d_skip.astype(jnp.float32)[:, None]
    y = y.reshape(bsz, t, D_INNER) * jax.nn.silu(z.astype(jnp.float32))
    y = rmsnorm(y.reshape(bsz, t, SSD_GROUPS, D_INNER // SSD_GROUPS),
                norm_g.reshape(SSD_GROUPS, D_INNER // SSD_GROUPS)).reshape(bsz, t, D_INNER)
    return y.astype(h.dtype) @ w_out, new_conv, h_new.astype(ssm_state.dtype)


def peer_ffn(h, wq, k1, k2, u_tab, v_tab):
    bsz, t, d = h.shape
    n = bsz * t
    nb = -(-n // TOK_BLOCK)
    xt = jnp.pad(h.reshape(n, d), ((0, nb * TOK_BLOCK - n), (0, 0))).reshape(nb, TOK_BLOCK, d)

    def block(xb):
        q = (xb @ wq).reshape(TOK_BLOCK, PEER_HEADS, D_KEY)
        s1 = jnp.einsum('thd,hkd->thk', q[..., :D_HALF], k1, preferred_element_type=jnp.float32)
        s2 = jnp.einsum('thd,hkd->thk', q[..., D_HALF:], k2, preferred_element_type=jnp.float32)
        v1, i1 = lax.top_k(s1, PEER_TOPK)
        v2, i2 = lax.top_k(s2, PEER_TOPK)
        cand = (v1[..., :, None] + v2[..., None, :]).reshape(TOK_BLOCK, PEER_HEADS, PEER_TOPK * PEER_TOPK)
        cidx = (i1[..., :, None] * N_KEYS + i2[..., None, :]).reshape(TOK_BLOCK, PEER_HEADS, PEER_TOPK * PEER_TOPK)
        best, sel = lax.top_k(cand, PEER_TOPK)
        eidx = jnp.take_along_axis(cidx, sel, axis=-1)
        gate = jax.nn.softmax(best, axis=-1)
        act = jax.nn.gelu(jnp.einsum('thkd,td->thk', u_tab[eidx], xb, preferred_element_type=jnp.float32),
                          approximate=False)
        return jnp.einsum('thk,thkd->td', (gate * act).astype(v_tab.dtype), v_tab[eidx])

    return lax.map(block, xt).reshape(nb * TOK_BLOCK, d)[:n].reshape(bsz, t, d)


def trunk(x, c, start_pos, past_k, past_v, pool_bufs, conv_bufs, ssm_states,
          ada_w, ada_b, norm_mix_g, norm_ffn_g, norm_out_g,
          ab_w_in, ab_w_out, pool_w, pool_scale, sb_bias,
          ssd_w_in, ssd_conv_w, ssd_conv_b, ssd_dt_bias, ssd_a_log, ssd_d, ssd_norm_g, ssd_w_out,
          peer_wq, peer_k1, peer_k2, peer_u, peer_v):
    new_k, new_v, new_pool, new_conv, new_ssm = [], [], [], [], []
    for l in range(DEPTH):
        mod = c @ ada_w[l] + ada_b[l]
        sh1, sc1, g1, sh2, sc2, g2 = jnp.split(mod, 6, axis=-1)
        h = modulate(x, norm_mix_g[l], sh1, sc1)
        j = l // 2
        if l % 2 == 0:
            pk = None if past_k is None else past_k[j]
            pv = None if past_v is None else past_v[j]
            y, k, v, nbuf = mixer_ab(h, start_pos, pool_bufs[j], pk, pv,
                                     ab_w_in[j], ab_w_out[j], pool_w[j], pool_scale[j], sb_bias[j])
            new_k.append(k)
            new_v.append(v)
            new_pool.append(nbuf)
        else:
            y, ncv, nss = mixer_ssd(h, conv_bufs[j], ssm_states[j], ssd_w_in[j], ssd_conv_w[j], ssd_conv_b[j],
                                    ssd_dt_bias[j], ssd_a_log[j], ssd_d[j], ssd_norm_g[j], ssd_w_out[j])
            new_conv.append(ncv)
            new_ssm.append(nss)
        x = x + g1[:, None, :] * y
        h = modulate(x, norm_ffn_g[l], sh2, sc2)
        x = x + g2[:, None, :] * peer_ffn(h, peer_wq[l], peer_k1[l], peer_k2[l], peer_u[l], peer_v[l])
    return (rmsnorm(x, norm_out_g), jnp.stack(new_k), jnp.stack(new_v), jnp.stack(new_pool),
            jnp.stack(new_conv), jnp.stack(new_ssm))


def setup_inputs(seed: int = 0) -> dict:
    key = jax.random.key(seed)
    ks = jax.random.split(key, 40)
    f32 = jnp.float32
    n_pages = PAST_LEN // PAGE_SIZE
    n_pool = (5 * DEC_BATCH * n_pages + 3) // 4

    def nrm(k, shape, s):
        return jax.random.normal(k, shape, f32) * s

    dt0 = jnp.exp(jax.random.uniform(ks[0], (N_C, SSD_HEADS), f32, math.log(1e-3), math.log(1e-1)))
    page_table = jax.random.permutation(ks[1], n_pool)[:DEC_BATCH * n_pages].reshape(DEC_BATCH, n_pages).astype(jnp.int32)
    return {
        'x_prompt': nrm(ks[2], (BATCH, SEQ, D_MODEL), 1.0),
        'x_sample': nrm(ks[3], (DEC_BATCH, DEC_SEQ, D_MODEL), 1.0),
        'cache_sb_k': nrm(ks[4], (N_AB, n_pool, PAGE_SIZE, SB_HEADS, SB_HEAD_DIM), 1.0),
        'cache_sb_v': nrm(ks[5], (N_AB, n_pool, PAGE_SIZE, SB_HEADS, SB_HEAD_DIM), 1.0),
        'state_pool': nrm(ks[6], (N_AB, DEC_BATCH, POOL_BUF, D_A), 1.0),
        'state_conv': nrm(ks[7], (N_C, DEC_BATCH, SSD_CONV - 1, CONV_DIM), 1.0),
        'state_ssm': nrm(ks[8], (N_C, DEC_BATCH, SSD_HEADS, SSD_HEAD_DIM, SSD_STATE), 0.5),
        'page_table': page_table,
        'c_prompt': nrm(ks[9], (BATCH, D_MODEL), 1.0),
        'c_sample': nrm(ks[10], (DEC_BATCH, D_MODEL), 1.0),
        'ada_w': nrm(ks[11], (DEPTH, D_MODEL, 6 * D_MODEL), 0.5 * D_MODEL ** -0.5),
        'ada_b': nrm(ks[12], (DEPTH, 6 * D_MODEL), 0.02),
        'norm_mix_g': 1.0 + nrm(ks[13], (DEPTH, D_MODEL), 0.05),
        'norm_ffn_g': 1.0 + nrm(ks[14], (DEPTH, D_MODEL), 0.05),
        'norm_out_g': 1.0 + nrm(ks[15], (D_MODEL,), 0.05),
        'ab_w_in': nrm(ks[16], (N_AB, D_MODEL, D_AB_IN), D_MODEL ** -0.5),
        'ab_w_out': nrm(ks[17], (N_AB, D_A + D_B, D_MODEL), (D_A + D_B) ** -0.5),
        'pool_w': nrm(ks[18], (N_AB, POOL_GROUPS, POOL_GC, POOL_GC), POOL_GC ** -0.5),
        'pool_scale': 1.0 + nrm(ks[19], (N_AB, D_A), 0.1),
        'sb_bias': jax.random.uniform(ks[32], (N_AB, SB_HEADS), f32, SB_BIAS_LO, SB_BIAS_HI),
        'ssd_w_in': nrm(ks[20], (N_C, D_MODEL, D_SSD_IN), D_MODEL ** -0.5),
        'ssd_conv_w': nrm(ks[21], (N_C, SSD_CONV, CONV_DIM), SSD_CONV ** -0.5),
        'ssd_conv_b': nrm(ks[22], (N_C, CONV_DIM), 0.02),
        'ssd_dt_bias': dt0 + jnp.log(-jnp.expm1(-dt0)),
        'ssd_a_log': jnp.log(jax.random.uniform(ks[23], (N_C, SSD_HEADS), f32, 1.0, 16.0)),
        'ssd_d': 1.0 + nrm(ks[24], (N_C, SSD_HEADS), 0.1),
        'ssd_norm_g': 1.0 + nrm(ks[25], (N_C, D_INNER), 0.05),
        'ssd_w_out': nrm(ks[26], (N_C, D_INNER, D_MODEL), D_INNER ** -0.5),
        'peer_wq': nrm(ks[27], (DEPTH, D_MODEL, PEER_HEADS * D_KEY), D_MODEL ** -0.5),
        'peer_k1': nrm(ks[28], (DEPTH, PEER_HEADS, N_KEYS, D_HALF), D_HALF ** -0.5),
        'peer_k2': nrm(ks[29], (DEPTH, PEER_HEADS, N_KEYS, D_HALF), D_HALF ** -0.5),
        'peer_u': nrm(ks[30], (DEPTH, N_EXPERTS, D_MODEL), D_MODEL ** -0.5),
        'peer_v': nrm(ks[31], (DEPTH, N_EXPERTS, D_MODEL), 0.5),
    }


def reference(x_prompt, x_sample, cache_sb_k, cache_sb_v, state_pool, state_conv, state_ssm, page_table,
              c_prompt, c_sample, ada_w, ada_b, norm_mix_g, norm_ffn_g, norm_out_g,
              ab_w_in, ab_w_out, pool_w, pool_scale, sb_bias,
              ssd_w_in, ssd_conv_w, ssd_conv_b, ssd_dt_bias, ssd_a_log, ssd_d, ssd_norm_g, ssd_w_out,
              peer_wq, peer_k1, peer_k2, peer_u, peer_v):
    bsz = x_prompt.shape[0]
    dbsz = x_sample.shape[0]
    pool0 = jnp.zeros((N_AB, bsz, POOL_BUF, D_A), x_prompt.dtype)
    conv0 = jnp.zeros((N_C, bsz, SSD_CONV - 1, CONV_DIM), x_prompt.dtype)
    ssm0 = jnp.zeros((N_C, bsz, SSD_HEADS, SSD_HEAD_DIM, SSD_STATE), jnp.float32)
    y_prompt, k_prompt, v_prompt, pool_prompt, conv_prompt, ssm_prompt = trunk(
        x_prompt, c_prompt, 0, None, None, pool0, conv0, ssm0,
        ada_w, ada_b, norm_mix_g, norm_ffn_g, norm_out_g, ab_w_in, ab_w_out, pool_w, pool_scale, sb_bias,
        ssd_w_in, ssd_conv_w, ssd_conv_b, ssd_dt_bias, ssd_a_log, ssd_d, ssd_norm_g, ssd_w_out,
        peer_wq, peer_k1, peer_k2, peer_u, peer_v)
    n_layers_ab = cache_sb_k.shape[0]
    past_k = cache_sb_k[:, page_table].reshape(n_layers_ab, dbsz, -1, SB_HEADS, SB_HEAD_DIM)
    past_v = cache_sb_v[:, page_table].reshape(n_layers_ab, dbsz, -1, SB_HEADS, SB_HEAD_DIM)
    y_sample, k_sample, v_sample, pool_sample, conv_sample, ssm_sample = trunk(
        x_sample, c_sample, PAST_LEN, past_k, past_v, state_pool, state_conv, state_ssm,
        ada_w, ada_b, norm_mix_g, norm_ffn_g, norm_out_g, ab_w_in, ab_w_out, pool_w, pool_scale, sb_bias,
        ssd_w_in, ssd_conv_w, ssd_conv_b, ssd_dt_bias, ssd_a_log, ssd_d, ssd_norm_g, ssd_w_out,
        peer_wq, peer_k1, peer_k2, peer_u, peer_v)
    return (y_prompt, y_sample, k_prompt, v_prompt, k_sample, v_sample,
            pool_prompt, pool_sample, conv_prompt, conv_sample, ssm_prompt, ssm_sample)
```

```python
import functools
import math

import jax
import jax.numpy as jnp
from jax import lax
from jax.experimental import pallas as pl
from jax.experimental.pallas import tpu as pltpu

F32 = jnp.float32
BF16 = jnp.bfloat16

D_MODEL = 1024
POOL_WINDOWS = (2, 4, 8, 16)
D_A = 256
POOL_GC = 64
POOL_BUF = 15
SB_HEAD_DIM = 64
D_B = 768
SB_HEADS = 12
D_INNER = 2048
SSD_HEAD_DIM = 64
SSD_HEADS = 32
SSD_GROUPS = 4
SSD_STATE = 128
SSD_CONV = 4
SSD_CHUNK = 128
CONV_DIM = 3072
D_SSD_IN = D_INNER + CONV_DIM + SSD_HEADS
PEER_HEADS = 8
N_KEYS = 128
N_EXPERTS = N_KEYS * N_KEYS
PEER_TOPK = 16
D_KEY = 256
D_HALF = 128
EPS = 1e-6

LANES = 128
SUBLANES = 8
VMEM_LIMIT = 56 * 1024 * 1024

NEG_INF = float("-inf")


def _cparams(sem):
    return pltpu.CompilerParams(dimension_semantics=sem, vmem_limit_bytes=VMEM_LIMIT)


def _modulate(x, g, shift, scale):
    r = lax.rsqrt(jnp.mean(x * x, axis=-1, keepdims=True) + EPS)
    return (x * r) * g * (1.0 + scale) + shift


def _split3(x):
    hi = x.astype(BF16)
    r1 = x - hi.astype(F32)
    mid = r1.astype(BF16)
    lo = (r1 - mid.astype(F32)).astype(BF16)
    return hi, mid, lo


def _dot_exact_rhs(a_bf16, x):
    hi, mid, lo = _split3(x)
    return (jnp.dot(a_bf16, hi, preferred_element_type=F32)
            + jnp.dot(a_bf16, mid, preferred_element_type=F32)
            + jnp.dot(a_bf16, lo, preferred_element_type=F32))


def _dot_exact_lhs(x, a_bf16):
    hi, mid, lo = _split3(x)
    return (jnp.dot(hi, a_bf16, preferred_element_type=F32)
            + jnp.dot(mid, a_bf16, preferred_element_type=F32)
            + jnp.dot(lo, a_bf16, preferred_element_type=F32))


def _mm_body(*refs, n_x, has_pro, has_bias, has_res):
    it = iter(refs)
    xs = [next(it) for _ in range(n_x)]
    ws = [next(it) for _ in range(n_x)]
    if has_pro:
        g_ref, sh_ref, sc_ref = next(it), next(it), next(it)
    if has_bias:
        b_ref = next(it)
    if has_res:
        r_ref, gt_ref = next(it), next(it)
    o_ref = next(it)
    acc = None
    for x_ref, w_ref in zip(xs, ws):
        x = x_ref[...]
        if has_pro:
            x = _modulate(x, g_ref[...], sh_ref[0], sc_ref[0])
        d = jnp.dot(x.astype(BF16), w_ref[...], preferred_element_type=F32)
        acc = d if acc is None else acc + d
    if has_bias:
        acc = acc + b_ref[...]
    if has_res:
        acc = r_ref[...] + gt_ref[0] * acc
    o_ref[...] = acc


def _group_spec(a, n_tiles):
    g, r, c = a.shape
    tiles_per_group = n_tiles // g
    return pl.BlockSpec((1, r, c), lambda i: (i // tiles_per_group, 0, 0))


def matmul(xs, ws, tm, pro=None, bias=None, res=None, name="mm"):
    m = xs[0].shape[0]
    n = ws[0].shape[1]
    nt = m // tm
    assert nt * tm == m
    in_specs = [pl.BlockSpec((tm, x.shape[1]), lambda i: (i, 0)) for x in xs]
    in_specs += [pl.BlockSpec(w.shape, lambda i: (0, 0)) for w in ws]
    args = list(xs) + list(ws)
    if pro is not None:
        g, sh, sc = pro
        in_specs += [pl.BlockSpec(g.shape, lambda i: (0, 0)), _group_spec(sh, nt), _group_spec(sc, nt)]
        args += [g, sh, sc]
    if bias is not None:
        in_specs.append(pl.BlockSpec(bias.shape, lambda i: (0, 0)))
        args.append(bias)
    if res is not None:
        r, gt = res
        in_specs += [pl.BlockSpec((tm, n), lambda i: (i, 0)), _group_spec(gt, nt)]
        args += [r, gt]
    body = functools.partial(_mm_body, n_x=len(xs), has_pro=pro is not None,
                             has_bias=bias is not None, has_res=res is not None)
    return pl.pallas_call(
        body,
        grid=(nt,),
        in_specs=in_specs,
        out_specs=pl.BlockSpec((tm, n), lambda i: (i, 0)),
        out_shape=jax.ShapeDtypeStruct((m, n), F32),
        compiler_params=_cparams(("arbitrary",)),
        name=name,
    )(*args)


POOL_HALO = 16


def _pool_body(cur_ref, prev_ref, buf_ref, wbd_ref, ps_ref, y_ref, nb_ref, win_ref, *, ch, start_pos):
    c = pl.program_id(1)
    cur = cur_ref[0]
    if ch >= POOL_HALO:
        tail = jnp.where(c == 0, buf_ref[0], prev_ref[0][ch - POOL_HALO:ch])
    else:
        tail = buf_ref[0]
    win_ref[0:POOL_HALO, :] = tail
    win_ref[POOL_HALO:POOL_HALO + ch, :] = cur

    def shifted(k):
        return win_ref[POOL_HALO - k:POOL_HALO - k + ch, :]

    pos = start_pos + c * ch + lax.broadcasted_iota(jnp.int32, (ch, 1), 0)
    grp = lax.broadcasted_iota(jnp.int32, (1, D_A), 1) // POOL_GC
    s = cur
    mean = jnp.zeros_like(cur)
    k = 1
    for g, w in enumerate(POOL_WINDOWS):
        while k < w:
            s = s + shifted(k)
            k += 1
        cnt = jnp.minimum(pos + 1, w).astype(F32)
        mean = jnp.where(grp == g, s / cnt, mean)
    diff = mean - cur
    y = jnp.dot(diff.astype(BF16), wbd_ref[...], preferred_element_type=F32) * ps_ref[...]
    y_ref[0] = y
    nb_ref[0] = win_ref[ch:ch + POOL_HALO, :]


def pool_mixer(proj3, buf16, wbd, pscale, start_pos):
    b, t, _ = proj3.shape
    ch = min(128, t)
    nc = t // ch
    body = functools.partial(_pool_body, ch=ch, start_pos=start_pos)
    return pl.pallas_call(
        body,
        grid=(b, nc),
        in_specs=[
            pl.BlockSpec((1, ch, D_A), lambda i, c: (i, c, 0)),
            pl.BlockSpec((1, ch, D_A), lambda i, c: (i, jnp.maximum(c - 1, 0), 0)),
            pl.BlockSpec((1, POOL_HALO, D_A), lambda i, c: (i, 0, 0)),
            pl.BlockSpec((D_A, D_A), lambda i, c: (0, 0)),
            pl.BlockSpec((1, D_A), lambda i, c: (0, 0)),
        ],
        out_specs=[
            pl.BlockSpec((1, ch, D_A), lambda i, c: (i, c, 0)),
            pl.BlockSpec((1, POOL_HALO, D_A), lambda i, c: (i, 0, 0)),
        ],
        out_shape=[
            jax.ShapeDtypeStruct((b, t, D_A), F32),
            jax.ShapeDtypeStruct((b, POOL_HALO, D_A), F32),
        ],
        scratch_shapes=[pltpu.VMEM((POOL_HALO + ch, D_A), F32)],
        compiler_params=_cparams(("arbitrary", "arbitrary")),
        name="pool",
    )(proj3, proj3, buf16, wbd, pscale)


def _cum_matrix(tk):
    r = lax.broadcasted_iota(jnp.int32, (tk, 2 * tk), 0)
    c = lax.broadcasted_iota(jnp.int32, (tk, 2 * tk), 1)
    return jnp.where((c >= tk) | (r > c), 1.0, 0.0).astype(BF16)


def _sb_weights(z, valid, cum, carry):
    tk = z.shape[1]
    l1p = jnp.log1p(jnp.exp(-jnp.abs(z)))
    log_beta = jnp.minimum(z, 0.0) - l1p
    log_keep = jnp.minimum(-z, 0.0) - l1p
    if valid is not None:
        log_keep = jnp.where(valid, log_keep, 0.0)
    hi = log_keep.astype(BF16)
    lo = (log_keep - hi.astype(F32)).astype(BF16)
    cs = (jnp.dot(hi, cum, preferred_element_type=F32)
          + jnp.dot(lo, cum, preferred_element_type=F32))
    after = cs[:, :tk] + carry
    w = jnp.exp(log_beta + after)
    if valid is not None:
        w = jnp.where(valid, w, 0.0)
    return w, carry + cs[:, tk:]


def _attn_prompt_body(bias_ref, q_ref, k_ref, v_ref, o_ref, acc_ref, car_ref, *, tq, tk):
    g = pl.program_id(1)
    i = pl.program_id(2)
    scale = SB_HEAD_DIM ** -0.5
    q = q_ref[0] * scale
    lane = lax.broadcasted_iota(jnp.int32, (1, LANES), 1)
    head_mask = [lane < SB_HEAD_DIM, lane >= SB_HEAD_DIM]
    qm = [jnp.where(m, q, 0.0).astype(BF16) for m in head_mask]
    bias = [bias_ref[2 * g], bias_ref[2 * g + 1]]
    cum = _cum_matrix(tk)
    acc_ref[...] = jnp.zeros_like(acc_ref)
    car_ref[...] = jnp.zeros_like(car_ref)
    qpos = i * tq + lax.broadcasted_iota(jnp.int32, (tq, tk), 0)
    kiota = lax.broadcasted_iota(jnp.int32, (tq, tk), 1)
    nk = (i * tq + tq) // tk

    def step(jj, _):
        j = nk - 1 - jj
        k0 = pl.multiple_of(j * tk, tk)
        kb = k_ref[0, pl.ds(k0, tk), :].astype(BF16)
        vb = v_ref[0, pl.ds(k0, tk), :]
        valid = (k0 + kiota) < qpos
        for h in range(2):
            z = lax.dot_general(qm[h], kb, (((1,), (1,)), ((), ())), preferred_element_type=F32) + bias[h]
            w, car = _sb_weights(z, valid, cum, car_ref[h])
            car_ref[h] = car
            vm = jnp.where(head_mask[h], vb, 0.0).astype(BF16)
            acc_ref[...] += jnp.dot(w.astype(BF16), vm, preferred_element_type=F32)
        return 0

    lax.fori_loop(0, nk, step, 0)
    o_ref[0] = acc_ref[...]


def attn_prompt(proj3, sb_bias, tq=128, tk=128):
    b, t, _ = proj3.shape
    tq = min(tq, t)
    tk = min(tk, t)
    n_pairs = SB_HEADS // 2
    qo = D_A // LANES
    body = functools.partial(_attn_prompt_body, tq=tq, tk=tk)
    return pl.pallas_call(
        body,
        grid=(b, n_pairs, t // tq),
        in_specs=[
            pl.BlockSpec(memory_space=pltpu.SMEM),
            pl.BlockSpec((1, tq, LANES), lambda bi, g, i: (bi, i, qo + g)),
            pl.BlockSpec((1, t, LANES), lambda bi, g, i: (bi, 0, qo + n_pairs + g)),
            pl.BlockSpec((1, t, LANES), lambda bi, g, i: (bi, 0, qo + 2 * n_pairs + g)),
        ],
        out_specs=pl.BlockSpec((1, tq, LANES), lambda bi, g, i: (bi, i, g)),
        scratch_shapes=[pltpu.VMEM((tq, LANES), F32), pltpu.VMEM((2, tq, tk), F32)],
        out_shape=jax.ShapeDtypeStruct((b, t, D_B), F32),
        compiler_params=_cparams(("arbitrary", "arbitrary", "arbitrary")),
        name="attn_prompt",
    )(sb_bias, proj3, proj3, proj3)


def _attn_sample_body(pt_ref, q_ref, kn_ref, vn_ref, kp_ref, vp_ref, bias_ref, o_ref,
                      qbd_ref, acc_ref, car_ref, *, t_new, page):
    p = pl.program_id(1)
    n_steps = pl.num_programs(1)
    rows = SB_HEADS * t_new
    row_head = lax.broadcasted_iota(jnp.int32, (rows, D_B), 0) // t_new
    col_head = lax.broadcasted_iota(jnp.int32, (rows, D_B), 1) // SB_HEAD_DIM
    blockdiag = row_head == col_head
    cum = _cum_matrix(page)

    def accumulate(kb, vb, valid):
        z = lax.dot_general(qbd_ref[...], kb.astype(BF16), (((1,), (1,)), ((), ())),
                            preferred_element_type=F32) + bias_ref[...]
        w, car = _sb_weights(z, valid, cum, car_ref[...])
        car_ref[...] = car
        acc_ref[...] += jnp.dot(w.astype(BF16), vb.astype(BF16), preferred_element_type=F32)

    @pl.when(p == 0)
    def _():
        scale = SB_HEAD_DIM ** -0.5
        q = q_ref[0] * scale
        qt = jnp.concatenate([q] * SB_HEADS, axis=0)
        qbd_ref[...] = jnp.where(blockdiag, qt, 0.0).astype(BF16)
        acc_ref[...] = jnp.zeros_like(acc_ref)
        car_ref[...] = jnp.zeros_like(car_ref)
        qi = lax.broadcasted_iota(jnp.int32, (rows, page), 0) % t_new
        ki = lax.broadcasted_iota(jnp.int32, (rows, page), 1)
        accumulate(kn_ref[0], vn_ref[0], ki < qi)

    @pl.when(p > 0)
    def _():
        accumulate(kp_ref[0, 0], vp_ref[0, 0], None)

    @pl.when(p == n_steps - 1)
    def _():
        a = jnp.where(blockdiag, acc_ref[...], 0.0)
        out = a[0:t_new]
        for h in range(1, SB_HEADS):
            out = out + a[h * t_new:(h + 1) * t_new]
        o_ref[0] = out


def attn_sample(q, k_new, v_new, cache_k, cache_v, page_table, bias_rows):
    b, t_new, _ = q.shape
    page = cache_k.shape[2]
    n_pages = page_table.shape[1]
    rows = SB_HEADS * t_new

    def page_map(bi, p, pt):
        return (0, pt[bi, n_pages - jnp.maximum(p, 1)], 0, 0)

    grid_spec = pltpu.PrefetchScalarGridSpec(
        num_scalar_prefetch=1,
        grid=(b, n_pages + 1),
        in_specs=[
            pl.BlockSpec((1, t_new, D_B), lambda bi, p, pt: (bi, 0, 0)),
            pl.BlockSpec((1, page, D_B), lambda bi, p, pt: (bi, 0, 0)),
            pl.BlockSpec((1, page, D_B), lambda bi, p, pt: (bi, 0, 0)),
            pl.BlockSpec((1, 1, page, D_B), page_map),
            pl.BlockSpec((1, 1, page, D_B), page_map),
            pl.BlockSpec((rows, 1), lambda bi, p, pt: (0, 0)),
        ],
        out_specs=pl.BlockSpec((1, t_new, D_B), lambda bi, p, pt: (bi, 0, 0)),
        scratch_shapes=[pltpu.VMEM((rows, D_B), BF16), pltpu.VMEM((rows, D_B), F32),
                        pltpu.VMEM((rows, page), F32)],
    )
    body = functools.partial(_attn_sample_body, t_new=t_new, page=page)
    return pl.pallas_call(
        body,
        grid_spec=grid_spec,
        out_shape=jax.ShapeDtypeStruct((b, t_new, D_B), F32),
        compiler_params=_cparams(("arbitrary", "arbitrary")),
        name="attn_sample",
    )(page_table, q, k_new, v_new, cache_k, cache_v, bias_rows)


CONV_HALO = 8
CONV_CB = 1024


def _conv_body(cur_ref, halo_ref, buf_ref, w_ref, b_ref, y_ref, nb_ref, win_ref, *, ch):
    c = pl.program_id(2)
    tail = jnp.where(c == 0, buf_ref[0], halo_ref[0])
    win_ref[0:CONV_HALO, :] = tail
    win_ref[CONV_HALO:CONV_HALO + ch, :] = cur_ref[0]
    w = w_ref[...]
    y = b_ref[...]
    for i in range(SSD_CONV):
        off = CONV_HALO - (SSD_CONV - 1) + i
        y = y + win_ref[off:off + ch, :] * w[i:i + 1, :]
    y_ref[0] = y * jax.nn.sigmoid(y)
    nb_ref[0] = win_ref[ch:ch + CONV_HALO, :]


def conv_silu(proj3, buf8, conv_w, conv_b):
    b, t, _ = proj3.shape
    ch = min(128, t)
    nc = t // ch
    col0 = D_INNER // CONV_CB
    ncb = CONV_DIM // CONV_CB
    hb = ch // CONV_HALO
    body = functools.partial(_conv_body, ch=ch)
    return pl.pallas_call(
        body,
        grid=(b, ncb, nc),
        in_specs=[
            pl.BlockSpec((1, ch, CONV_CB), lambda i, j, c: (i, c, col0 + j)),
            pl.BlockSpec((1, CONV_HALO, CONV_CB), lambda i, j, c: (i, jnp.maximum(c * hb - 1, 0), col0 + j)),
            pl.BlockSpec((1, CONV_HALO, CONV_CB), lambda i, j, c: (i, 0, j)),
            pl.BlockSpec((SSD_CONV, CONV_CB), lambda i, j, c: (0, j)),
            pl.BlockSpec((1, CONV_CB), lambda i, j, c: (0, j)),
        ],
        out_specs=[
            pl.BlockSpec((1, ch, CONV_CB), lambda i, j, c: (i, c, j)),
            pl.BlockSpec((1, CONV_HALO, CONV_CB), lambda i, j, c: (i, 0, j)),
        ],
        out_shape=[
            jax.ShapeDtypeStruct((b, t, CONV_DIM), F32),
            jax.ShapeDtypeStruct((b, CONV_HALO, CONV_DIM), F32),
        ],
        scratch_shapes=[pltpu.VMEM((CONV_HALO + ch, CONV_CB), F32)],
        compiler_params=_cparams(("arbitrary", "arbitrary", "arbitrary")),
        name="conv",
    )(proj3, proj3, buf8, conv_w, conv_b)


def _ssd_body(x_ref, b_ref, c_ref, z_ref, dt_ref, dtb_ref, alog_ref, d_ref, ng_ref, h0_ref,
              y_ref, hout_ref, st_ref, *, cl, t_valid):
    c = pl.program_id(1)
    nc = pl.num_programs(1)
    n_pairs = SSD_HEADS // 2
    pairs_per_group = n_pairs // SSD_GROUPS

    @pl.when(c == 0)
    def _():
        for m in range(n_pairs):
            st_ref[m] = h0_ref[0, m].T

    row = lax.broadcasted_iota(jnp.int32, (cl, LANES), 0)
    lane = lax.broadcasted_iota(jnp.int32, (cl, LANES), 1)
    dt = jax.nn.softplus(dt_ref[0] + dtb_ref[...])
    dt = jnp.where((c * cl + row < t_valid) & (lane < SSD_HEADS), dt, 0.0)
    a = -jnp.exp(alog_ref[...])
    da = dt * a
    tri_r = lax.broadcasted_iota(jnp.int32, (cl, cl), 0)
    tri_c = lax.broadcasted_iota(jnp.int32, (cl, cl), 1)
    causal = tri_r >= tri_c
    ltri = jnp.where(causal, 1.0, 0.0).astype(BF16)
    acs = _dot_exact_rhs(ltri, da)
    acs_t = acs.T
    er = lax.broadcasted_iota(jnp.int32, (LANES, D_INNER), 0)
    ec = lax.broadcasted_iota(jnp.int32, (LANES, D_INNER), 1) // SSD_HEAD_DIM
    expand = jnp.where(er == ec, 1.0, 0.0).astype(BF16)
    dt_full = _dot_exact_lhs(dt, expand)
    acs_full = _dot_exact_lhs(acs, expand)
    last_full = acs_full[cl - 1:cl, :]
    xall = x_ref[0]
    xdt = xall * dt_full
    xdt_end = (xdt * jnp.exp(last_full - acs_full)).astype(BF16)
    xdt_b = xdt.astype(BF16)
    e_acs = jnp.exp(acs_full)
    chunk_decay = jnp.exp(last_full)
    lane_lo = lax.broadcasted_iota(jnp.int32, (1, LANES), 1) < SSD_HEAD_DIM
    ys = []
    for g in range(SSD_GROUPS):
        bg = b_ref[0][:, g * SSD_STATE:(g + 1) * SSD_STATE]
        cg = c_ref[0][:, g * SSD_STATE:(g + 1) * SSD_STATE].astype(BF16)
        cb = lax.dot_general(cg, bg.astype(BF16), (((1,), (1,)), ((), ())), preferred_element_type=F32)
        bg_t = bg.T.astype(BF16)
        for mm in range(pairs_per_group):
            m = g * pairs_per_group + mm
            sl = slice(m * LANES, (m + 1) * LANES)
            yd = None
            for hh in range(2):
                h = 2 * m + hh
                seg = acs[:, h:h + 1] - acs_t[h:h + 1, :]
                decay = jnp.exp(jnp.where(causal, seg, NEG_INF))
                sc = (cb * decay).astype(BF16)
                xm = jnp.where(lane_lo if hh == 0 else ~lane_lo, xdt_b[:, sl], jnp.zeros((), BF16))
                d = jnp.dot(sc, xm, preferred_element_type=F32)
                yd = d if yd is None else yd + d
            st = st_ref[m]
            yo = jnp.dot(cg, st.astype(BF16), preferred_element_type=F32) * e_acs[:, sl]
            cs = jnp.dot(bg_t, xdt_end[:, sl], preferred_element_type=F32)
            st_ref[m] = st * chunk_decay[:, sl] + cs
            ys.append(yd + yo)
    y = jnp.concatenate(ys, axis=1) + xall * d_ref[...]
    zz = z_ref[0]
    y = y * (zz * jax.nn.sigmoid(zz))
    gw = D_INNER // SSD_GROUPS
    outs = []
    for g in range(SSD_GROUPS):
        s = y[:, g * gw:(g + 1) * gw]
        r = lax.rsqrt(jnp.mean(s * s, axis=-1, keepdims=True) + EPS)
        outs.append(s * r)
    y_ref[0] = jnp.concatenate(outs, axis=1) * ng_ref[...]

    @pl.when(c == nc - 1)
    def _():
        for m in range(n_pairs):
            hout_ref[0, m] = st_ref[m].T


def ssd_scan(xact, proj3, dt_bias, a_log, d_full, norm_g, h0, t_valid):
    b, t, _ = xact.shape
    cl = SSD_CHUNK
    nc = t // cl
    gn = SSD_GROUPS * SSD_STATE
    body = functools.partial(_ssd_body, cl=cl, t_valid=t_valid)
    n_pairs = SSD_HEADS // 2
    return pl.pallas_call(
        body,
        grid=(b, nc),
        in_specs=[
            pl.BlockSpec((1, cl, D_INNER), lambda i, c: (i, c, 0)),
            pl.BlockSpec((1, cl, gn), lambda i, c: (i, c, D_INNER // gn)),
            pl.BlockSpec((1, cl, gn), lambda i, c: (i, c, D_INNER // gn + 1)),
            pl.BlockSpec((1, cl, D_INNER), lambda i, c: (i, c, 0)),
            pl.BlockSpec((1, cl, LANES), lambda i, c: (i, c, (D_INNER + CONV_DIM) // LANES)),
            pl.BlockSpec((1, LANES), lambda i, c: (0, 0)),
            pl.BlockSpec((1, LANES), lambda i, c: (0, 0)),
            pl.BlockSpec((1, D_INNER), lambda i, c: (0, 0)),
            pl.BlockSpec((1, D_INNER), lambda i, c: (0, 0)),
            pl.BlockSpec((1, n_pairs, LANES, SSD_STATE), lambda i, c: (i, 0, 0, 0)),
        ],
        out_specs=[
            pl.BlockSpec((1, cl, D_INNER), lambda i, c: (i, c, 0)),
            pl.BlockSpec((1, n_pairs, LANES, SSD_STATE), lambda i, c: (i, 0, 0, 0)),
        ],
        out_shape=[
            jax.ShapeDtypeStruct((b, t, D_INNER), F32),
            jax.ShapeDtypeStruct((b, n_pairs, LANES, SSD_STATE), F32),
        ],
        scratch_shapes=[pltpu.VMEM((n_pairs, SSD_STATE, LANES), F32)],
        compiler_params=_cparams(("arbitrary", "arbitrary")),
        name="ssd_scan",
    )(xact, xact, xact, proj3, proj3, dt_bias, a_log, d_full, norm_g, h0)


def _top_rows(v, k):
    n = v.shape[0]
    rows = lax.broadcasted_iota(jnp.int32, v.shape, 0)
    outs = []
    for _ in range(k):
        m = jnp.max(v, axis=0, keepdims=True)
        outs.append(m)
        first = jnp.min(jnp.where(v == m, rows, n), axis=0, keepdims=True)
        v = jnp.where(rows == first, NEG_INF, v)
    return jnp.concatenate(outs, axis=0)


def _peer_body(x_ref, g_ref, sh_ref, sc_ref, gt_ref, wq_ref, k1_ref, k2_ref, u_ref, vt_ref, og_ref,
               o_ref, ht_ref, s1_ref, s2_ref, f1_ref, f2_ref, tau_ref, acc_ref, *, tt, eb, final_norm):
    j = pl.program_id(1)
    nj = pl.num_programs(1)
    ncol = tt // LANES

    @pl.when(j == 0)
    def _():
        h = _modulate(x_ref[...], g_ref[...], sh_ref[0], sc_ref[0])
        ht_ref[...] = h.T.astype(BF16)
        acc_ref[...] = jnp.zeros_like(acc_ref)

        def per_head(hd, _):
            r0 = pl.multiple_of(hd * D_KEY, D_KEY)
            qh = jnp.dot(wq_ref[pl.ds(r0, D_KEY), :], ht_ref[...], preferred_element_type=F32)
            s1 = jnp.dot(k1_ref[hd], qh[:D_HALF].astype(BF16), preferred_element_type=F32)
            s2 = jnp.dot(k2_ref[hd], qh[D_HALF:].astype(BF16), preferred_element_type=F32)
            v1 = _top_rows(s1, PEER_TOPK)
            v2 = _top_rows(s2, PEER_TOPK)
            cands = []
            for a_ in range(PEER_TOPK):
                nb = PEER_TOPK // (a_ + 1)
                cands.append(v1[a_:a_ + 1] + v2[0:nb])
            cand = jnp.concatenate(cands, axis=0)
            best = _top_rows(cand, PEER_TOPK)
            zsum = jnp.sum(jnp.exp(best - best[0:1]), axis=0, keepdims=True)
            f1 = jnp.exp(s1 - v1[0:1]) / zsum
            f2 = jnp.exp(s2 - v2[0:1])
            tau_ref[hd] = best[PEER_TOPK - 1:PEER_TOPK]
            for cc in range(ncol):
                cs = slice(cc * LANES, (cc + 1) * LANES)
                s1_ref[hd, cc] = s1[:, cs]
                s2_ref[hd, cc] = s2[:, cs]
                f1_ref[hd, cc] = f1[:, cs]
                f2_ref[hd, cc] = f2[:, cs]
            return 0

        lax.fori_loop(0, PEER_HEADS, per_head, 0)

    a = jnp.dot(u_ref[...], ht_ref[...], preferred_element_type=F32)
    act = 0.5 * a * (1.0 + lax.erf(a * (2.0 ** -0.5)))
    n_i1 = eb // N_KEYS
    blocks = []
    for ii in range(n_i1):
        i1 = j * n_i1 + ii
        cols = []
        for cc in range(ncol):
            w = jnp.zeros((N_KEYS, LANES), F32)
            for hd in range(PEER_HEADS):
                s1r = s1_ref[hd, cc, pl.ds(i1, 1), :]
                f1r = f1_ref[hd, cc, pl.ds(i1, 1), :]
                tau = tau_ref[hd, :, cc * LANES:(cc + 1) * LANES]
                sel = (s1r + s2_ref[hd, cc]) >= tau
                w = w + jnp.where(sel, f1r * f2_ref[hd, cc], 0.0)
            cols.append(w)
        blocks.append(jnp.concatenate(cols, axis=1))
    wfull = jnp.concatenate(blocks, axis=0)
    p = (wfull * act).astype(BF16)
    acc_ref[...] += jnp.dot(vt_ref[...], p, preferred_element_type=F32)

    @pl.when(j == nj - 1)
    def _():
        y = x_ref[...] + gt_ref[0] * acc_ref[...].T
        if final_norm:
            r = lax.rsqrt(jnp.mean(y * y, axis=-1, keepdims=True) + EPS)
            y = (y * r) * og_ref[...]
        o_ref[...] = y


def peer_layer(x, g, sh, sc, gt, wq_t, k1, k2, u, vt, out_g, tt, eb=512, final_norm=False):
    n = x.shape[0]
    nt = n // tt
    ncol = tt // LANES
    nj = N_EXPERTS // eb
    body = functools.partial(_peer_body, tt=tt, eb=eb, final_norm=final_norm)

    def grp(a):
        gg, r, c = a.shape
        tpg = nt // gg
        return pl.BlockSpec((1, r, c), lambda i, j: (i // tpg, 0, 0))

    return pl.pallas_call(
        body,
        grid=(nt, nj),
        in_specs=[
            pl.BlockSpec((tt, D_MODEL), lambda i, j: (i, 0)),
            pl.BlockSpec((1, D_MODEL), lambda i, j: (0, 0)),
            grp(sh), grp(sc), grp(gt),
            pl.BlockSpec(wq_t.shape, lambda i, j: (0, 0)),
            pl.BlockSpec(k1.shape, lambda i, j: (0, 0, 0)),
            pl.BlockSpec(k2.shape, lambda i, j: (0, 0, 0)),
            pl.BlockSpec((eb, D_MODEL), lambda i, j: (j, 0)),
            pl.BlockSpec((D_MODEL, eb), lambda i, j: (0, j)),
            pl.BlockSpec((1, D_MODEL), lambda i, j: (0, 0)),
        ],
        out_specs=pl.BlockSpec((tt, D_MODEL), lambda i, j: (i, 0)),
        out_shape=jax.ShapeDtypeStruct((n, D_MODEL), F32),
        scratch_shapes=[
            pltpu.VMEM((D_MODEL, tt), BF16),
            pltpu.VMEM((PEER_HEADS, ncol, N_KEYS, LANES), F32),
            pltpu.VMEM((PEER_HEADS, ncol, N_KEYS, LANES), F32),
            pltpu.VMEM((PEER_HEADS, ncol, N_KEYS, LANES), F32),
            pltpu.VMEM((PEER_HEADS, ncol, N_KEYS, LANES), F32),
            pltpu.VMEM((PEER_HEADS, 1, tt), F32),
            pltpu.VMEM((D_MODEL, tt), F32),
        ],
        compiler_params=_cparams(("arbitrary", "arbitrary")),
        name="peer",
    )(x, g, sh, sc, gt, wq_t, k1, k2, u, vt, out_g)


def _row_groups(a, bsz, t, tm):
    if t % tm == 0:
        return a[:, None, :]
    assert tm % t == 0
    per_row = jnp.repeat(a, t, axis=0)
    return per_row.reshape(bsz * t // tm, tm, a.shape[1])


def _trunk(x, c_mod, start_pos, past, pool_buf, conv_buf, ssm_state, wts, tm, tt):
    bsz, t, _ = x.shape
    n = bsz * t
    x2 = x.reshape(n, D_MODEL)
    rg = lambda a: _row_groups(a, bsz, t, tm)
    rgp = lambda a: _row_groups(a, bsz, t, tt)

    sh1, sc1, g1, sh2, sc2, g2 = c_mod[0]
    proj = matmul([x2], [wts["ab_w_in"]], tm, pro=(wts["norm_mix_g"][0], rg(sh1), rg(sc1)), name="ab_in")
    proj3 = proj.reshape(bsz, t, -1)
    k_new = proj3[..., D_A + D_B:D_A + 2 * D_B]
    v_new = proj3[..., D_A + 2 * D_B:]
    buf16 = jnp.pad(pool_buf, ((0, 0), (POOL_HALO - POOL_BUF, 0), (0, 0)))
    ya, nb16 = pool_mixer(proj3, buf16, wts["pool_wbd"], wts["pool_scale"], start_pos)
    new_pool = nb16[:, POOL_HALO - POOL_BUF:]
    if past is None:
        yb = attn_prompt(proj3, wts["sb_bias"])
    else:
        cache_k, cache_v, page_table = past
        page = cache_k.shape[2]
        q = proj3[..., D_A:D_A + D_B]
        padr = ((0, 0), (0, page - t), (0, 0))
        bias_rows = jnp.repeat(wts["sb_bias"], t)[:, None]
        yb = attn_sample(q, jnp.pad(k_new, padr), jnp.pad(v_new, padr), cache_k, cache_v, page_table, bias_rows)
    x2 = matmul([ya.reshape(n, D_A), yb.reshape(n, D_B)], [wts["ab_w_out_a"], wts["ab_w_out_b"]], tm,
                res=(x2, rg(g1)), name="ab_out")
    x2 = peer_layer(x2, wts["norm_ffn_g"][0], rgp(sh2), rgp(sc2), rgp(g2), wts["peer_wq_t"][0],
                    wts["peer_k1"][0], wts["peer_k2"][0], wts["peer_u"][0], wts["peer_vt"][0],
                    wts["norm_out_g"], tt)

    sh1, sc1, g1, sh2, sc2, g2 = c_mod[1]
    proj = matmul([x2], [wts["ssd_w_in"]], tm, pro=(wts["norm_mix_g"][1], rg(sh1), rg(sc1)), name="ssd_in")
    proj3 = proj.reshape(bsz, t, -1)
    buf8 = jnp.pad(conv_buf, ((0, 0), (CONV_HALO - (SSD_CONV - 1), 0), (0, 0)))
    xact, nc8 = conv_silu(proj3, buf8, wts["ssd_conv_w"], wts["ssd_conv_b"])
    new_conv = nc8[:, CONV_HALO - (SSD_CONV - 1):]
    tp = -(-t // SSD_CHUNK) * SSD_CHUNK
    if tp != t:
        padt = ((0, 0), (0, tp - t), (0, 0))
        xact = jnp.pad(xact, padt)
        proj3 = jnp.pad(proj3, padt)
    h0 = ssm_state.reshape(bsz, SSD_HEADS // 2, LANES, SSD_STATE)
    ynorm, hlast = ssd_scan(xact, proj3, wts["ssd_dt_bias"], wts["ssd_a_log"], wts["ssd_d_full"],
                            wts["ssd_norm_g"], h0, t)
    new_ssm = hlast.reshape(bsz, SSD_HEADS, SSD_HEAD_DIM, SSD_STATE)
    ynorm = ynorm[:, :t].reshape(n, D_INNER)
    x2 = matmul([ynorm], [wts["ssd_w_out"]], tm, res=(x2, rg(g1)), name="ssd_out")
    x2 = peer_layer(x2, wts["norm_ffn_g"][1], rgp(sh2), rgp(sc2), rgp(g2), wts["peer_wq_t"][1],
                    wts["peer_k1"][1], wts["peer_k2"][1], wts["peer_u"][1], wts["peer_vt"][1],
                    wts["norm_out_g"], tt, final_norm=True)
    return (x2.reshape(bsz, t, D_MODEL), k_new.reshape(bsz, t, SB_HEADS, SB_HEAD_DIM)[None],
            v_new.reshape(bsz, t, SB_HEADS, SB_HEAD_DIM)[None], new_pool[None], new_conv[None], new_ssm[None])


def kernel(x_prompt, x_sample, cache_sb_k, cache_sb_v, state_pool, state_conv, state_ssm, page_table,
           c_prompt, c_sample, ada_w, ada_b, norm_mix_g, norm_ffn_g, norm_out_g,
           ab_w_in, ab_w_out, pool_w, pool_scale, sb_bias,
           ssd_w_in, ssd_conv_w, ssd_conv_b, ssd_dt_bias, ssd_a_log, ssd_d, ssd_norm_g, ssd_w_out,
           peer_wq, peer_k1, peer_k2, peer_u, peer_v):
    bsz, seq, _ = x_prompt.shape
    dbsz, dseq, _ = x_sample.shape
    depth = ada_w.shape[0]
    assert depth == 2 and ab_w_in.shape[0] == 1 and ssd_w_in.shape[0] == 1

    ssd_pad = (-D_SSD_IN) % LANES
    pad_lanes = lambda a: jnp.pad(a, ((0, 0), (0, LANES - a.shape[1])))
    wts = {
        "norm_mix_g": [norm_mix_g[l][None] for l in range(depth)],
        "norm_ffn_g": [norm_ffn_g[l][None] for l in range(depth)],
        "norm_out_g": norm_out_g[None],
        "ab_w_in": ab_w_in[0].astype(BF16),
        "ab_w_out_a": ab_w_out[0, :D_A].astype(BF16),
        "ab_w_out_b": ab_w_out[0, D_A:].astype(BF16),
        "pool_wbd": jax.scipy.linalg.block_diag(*[pool_w[0, g] for g in range(len(POOL_WINDOWS))]).astype(BF16),
        "pool_scale": pool_scale[0][None],
        "sb_bias": sb_bias[0],
        "ssd_w_in": jnp.pad(ssd_w_in[0], ((0, 0), (0, ssd_pad))).astype(BF16),
        "ssd_conv_w": ssd_conv_w[0],
        "ssd_conv_b": ssd_conv_b[0][None],
        "ssd_dt_bias": pad_lanes(ssd_dt_bias[0][None]),
        "ssd_a_log": pad_lanes(ssd_a_log[0][None]),
        "ssd_d_full": jnp.repeat(ssd_d[0], SSD_HEAD_DIM)[None],
        "ssd_norm_g": ssd_norm_g[0][None],
        "ssd_w_out": ssd_w_out[0].astype(BF16),
        "peer_wq_t": [peer_wq[l].T.astype(BF16) for l in range(depth)],
        "peer_k1": [peer_k1[l].astype(BF16) for l in range(depth)],
        "peer_k2": [peer_k2[l].astype(BF16) for l in range(depth)],
        "peer_u": [peer_u[l].astype(BF16) for l in range(depth)],
        "peer_vt": [peer_v[l].T.astype(BF16) for l in range(depth)],
    }

    c_all = jnp.concatenate([c_prompt, c_sample], axis=0)
    mods_p, mods_s = [], []
    for l in range(depth):
        mod = matmul([c_all], [ada_w[l].astype(BF16)], c_all.shape[0], bias=ada_b[l][None], name="ada")
        parts = jnp.split(mod, 6, axis=-1)
        mods_p.append([p[:bsz] for p in parts])
        mods_s.append([p[bsz:] for p in parts])

    zeros = lambda shape: jnp.zeros(shape, F32)
    out_p = _trunk(x_prompt, mods_p, 0, None, zeros((bsz, POOL_BUF, D_A)), zeros((bsz, SSD_CONV - 1, CONV_DIM)),
                   zeros((bsz, SSD_HEADS, SSD_HEAD_DIM, SSD_STATE)), wts, tm=256, tt=512)
    n_pool, page = cache_sb_k.shape[1], cache_sb_k.shape[2]
    past = (cache_sb_k.reshape(1, n_pool, page, D_B), cache_sb_v.reshape(1, n_pool, page, D_B), page_table)
    past_len = page_table.shape[1] * page
    out_s = _trunk(x_sample, mods_s, past_len, past, state_pool[0], state_conv[0], state_ssm[0], wts,
                   tm=dbsz * dseq, tt=dbsz * dseq)
    y_p, k_p, v_p, pool_p, conv_p, ssm_p = out_p
    y_s, k_s, v_s, pool_s, conv_s, ssm_s = out_s
    return (y_p, y_s, k_p, v_p, k_s, v_s, pool_p, pool_s, conv_p, conv_s, ssm_p, ssm_s)
```

```python
import functools
import math

import jax
import jax.numpy as jnp
from jax import lax
from jax.experimental import pallas as pl
from jax.experimental.pallas import tpu as pltpu

F32 = jnp.float32
BF16 = jnp.bfloat16

D_MODEL = 1024
POOL_WINDOWS = (2, 4, 8, 16)
D_A = 256
POOL_GC = 64
POOL_BUF = 15
SB_HEAD_DIM = 64
D_B = 768
SB_HEADS = 12
D_INNER = 2048
SSD_HEAD_DIM = 64
SSD_HEADS = 32
SSD_GROUPS = 4
SSD_STATE = 128
SSD_CONV = 4
SSD_CHUNK = 128
CONV_DIM = 3072
D_SSD_IN = D_INNER + CONV_DIM + SSD_HEADS
PEER_HEADS = 8
N_KEYS = 128
N_EXPERTS = N_KEYS * N_KEYS
PEER_TOPK = 16
D_KEY = 256
D_HALF = 128
EPS = 1e-6

LANES = 128
SUBLANES = 8
VMEM_LIMIT = 56 * 1024 * 1024

NEG_INF = float("-inf")


def _cparams(sem):
    return pltpu.CompilerParams(dimension_semantics=sem, vmem_limit_bytes=VMEM_LIMIT)


def _modulate(x, g, shift, scale):
    r = lax.rsqrt(jnp.mean(x * x, axis=-1, keepdims=True) + EPS)
    return (x * r) * g * (1.0 + scale) + shift


def _split3(x):
    hi = x.astype(BF16)
    r1 = x - hi.astype(F32)
    mid = r1.astype(BF16)
    lo = (r1 - mid.astype(F32)).astype(BF16)
    return hi, mid, lo


def _dot_exact_rhs(a_bf16, x):
    hi, mid, lo = _split3(x)
    return (jnp.dot(a_bf16, hi, preferred_element_type=F32)
            + jnp.dot(a_bf16, mid, preferred_element_type=F32)
            + jnp.dot(a_bf16, lo, preferred_element_type=F32))


def _dot_exact_lhs(x, a_bf16):
    hi, mid, lo = _split3(x)
    return (jnp.dot(hi, a_bf16, preferred_element_type=F32)
            + jnp.dot(mid, a_bf16, preferred_element_type=F32)
            + jnp.dot(lo, a_bf16, preferred_element_type=F32))


def _mm_body(*refs, n_x, has_pro, has_bias, has_res):
    it = iter(refs)
    xs = [next(it) for _ in range(n_x)]
    ws = [next(it) for _ in range(n_x)]
    if has_pro:
        g_ref, sh_ref, sc_ref = next(it), next(it), next(it)
    if has_bias:
        b_ref = next(it)
    if has_res:
        r_ref, gt_ref = next(it), next(it)
    o_ref = next(it)
    acc = None
    for x_ref, w_ref in zip(xs, ws):
        x = x_ref[...]
        if has_pro:
            x = _modulate(x, g_ref[...], sh_ref[0], sc_ref[0])
        d = jnp.dot(x.astype(BF16), w_ref[...], preferred_element_type=F32)
        acc = d if acc is None else acc + d
    if has_bias:
        acc = acc + b_ref[...]
    if has_res:
        acc = r_ref[...] + gt_ref[0] * acc
    o_ref[...] = acc


def _group_spec(a, n_tiles):
    g, r, c = a.shape
    tiles_per_group = n_tiles // g
    return pl.BlockSpec((1, r, c), lambda i: (i // tiles_per_group, 0, 0))


def matmul(xs, ws, tm, pro=None, bias=None, res=None, name="mm"):
    m = xs[0].shape[0]
    n = ws[0].shape[1]
    nt = m // tm
    assert nt * tm == m
    in_specs = [pl.BlockSpec((tm, x.shape[1]), lambda i: (i, 0)) for x in xs]
    in_specs += [pl.BlockSpec(w.shape, lambda i: (0, 0)) for w in ws]
    args = list(xs) + list(ws)
    if pro is not None:
        g, sh, sc = pro
        in_specs += [pl.BlockSpec(g.shape, lambda i: (0, 0)), _group_spec(sh, nt), _group_spec(sc, nt)]
        args += [g, sh, sc]
    if bias is not None:
        in_specs.append(pl.BlockSpec(bias.shape, lambda i: (0, 0)))
        args.append(bias)
    if res is not None:
        r, gt = res
        in_specs += [pl.BlockSpec((tm, n), lambda i: (i, 0)), _group_spec(gt, nt)]
        args += [r, gt]
    body = functools.partial(_mm_body, n_x=len(xs), has_pro=pro is not None,
                             has_bias=bias is not None, has_res=res is not None)
    return pl.pallas_call(
        body,
        grid=(nt,),
        in_specs=in_specs,
        out_specs=pl.BlockSpec((tm, n), lambda i: (i, 0)),
        out_shape=jax.ShapeDtypeStruct((m, n), F32),
        compiler_params=_cparams(("arbitrary",)),
        name=name,
    )(*args)


POOL_HALO = 16


def _pool_body(cur_ref, prev_ref, buf_ref, wbd_ref, ps_ref, y_ref, nb_ref, win_ref, *, ch, start_pos):
    c = pl.program_id(1)
    cur = cur_ref[0]
    if ch >= POOL_HALO:
        tail = jnp.where(c == 0, buf_ref[0], prev_ref[0][ch - POOL_HALO:ch])
    else:
        tail = buf_ref[0]
    win_ref[0:POOL_HALO, :] = tail
    win_ref[POOL_HALO:POOL_HALO + ch, :] = cur

    def shifted(k):
        return win_ref[POOL_HALO - k:POOL_HALO - k + ch, :]

    pos = start_pos + c * ch + lax.broadcasted_iota(jnp.int32, (ch, 1), 0)
    grp = lax.broadcasted_iota(jnp.int32, (1, D_A), 1) // POOL_GC
    s = cur
    mean = jnp.zeros_like(cur)
    k = 1
    for g, w in enumerate(POOL_WINDOWS):
        while k < w:
            s = s + shifted(k)
            k += 1
        cnt = jnp.minimum(pos + 1, w).astype(F32)
        mean = jnp.where(grp == g, s / cnt, mean)
    diff = mean - cur
    y = jnp.dot(diff.astype(BF16), wbd_ref[...], preferred_element_type=F32) * ps_ref[...]
    y_ref[0] = y
    nb_ref[0] = win_ref[ch:ch + POOL_HALO, :]


def pool_mixer(proj3, buf16, wbd, pscale, start_pos):
    b, t, _ = proj3.shape
    ch = min(128, t)
    nc = t // ch
    body = functools.partial(_pool_body, ch=ch, start_pos=start_pos)
    return pl.pallas_call(
        body,
        grid=(b, nc),
        in_specs=[
            pl.BlockSpec((1, ch, D_A), lambda i, c: (i, c, 0)),
            pl.BlockSpec((1, ch, D_A), lambda i, c: (i, jnp.maximum(c - 1, 0), 0)),
            pl.BlockSpec((1, POOL_HALO, D_A), lambda i, c: (i, 0, 0)),
            pl.BlockSpec((D_A, D_A), lambda i, c: (0, 0)),
            pl.BlockSpec((1, D_A), lambda i, c: (0, 0)),
        ],
        out_specs=[
            pl.BlockSpec((1, ch, D_A), lambda i, c: (i, c, 0)),
            pl.BlockSpec((1, POOL_HALO, D_A), lambda i, c: (i, 0, 0)),
        ],
        out_shape=[
            jax.ShapeDtypeStruct((b, t, D_A), F32),
            jax.ShapeDtypeStruct((b, POOL_HALO, D_A), F32),
        ],
        scratch_shapes=[pltpu.VMEM((POOL_HALO + ch, D_A), F32)],
        compiler_params=_cparams(("arbitrary", "arbitrary")),
        name="pool",
    )(proj3, proj3, buf16, wbd, pscale)


SAMPLE_PAGES_PER_STEP = 4


def _cum_matrix(tk, n_ones, ones_lo, ones_hi):
    r = lax.broadcasted_iota(jnp.int32, (tk, tk + n_ones), 0)
    c = lax.broadcasted_iota(jnp.int32, (tk, tk + n_ones), 1)
    ones = (c >= tk + ones_lo) & (c < tk + ones_hi)
    return jnp.where(ones | ((c < tk) & (r > c)), 1.0, 0.0).astype(BF16)


def _sb_block(z, valid, cum):
    tk = z.shape[1]
    l1p = jnp.log1p(jnp.exp(-jnp.abs(z)))
    log_beta = jnp.minimum(z, 0.0) - l1p
    log_keep = jnp.minimum(-z, 0.0) - l1p
    if valid is not None:
        log_keep = jnp.where(valid, log_keep, 0.0)
    hi = log_keep.astype(BF16)
    lo = (log_keep - hi.astype(F32)).astype(BF16)
    cs = (jnp.dot(hi, cum, preferred_element_type=F32)
          + jnp.dot(lo, cum, preferred_element_type=F32))
    w = jnp.exp(log_beta + cs[:, :tk])
    if valid is not None:
        w = jnp.where(valid, w, 0.0)
    return w, cs[:, tk:]


def _attn_prompt_body(bias_ref, q_ref, k_ref, v_ref, o_ref, acc_ref, car_ref, *, tq, tk):
    g = pl.program_id(1)
    i = pl.program_id(2)
    scale = SB_HEAD_DIM ** -0.5
    q = q_ref[0] * scale
    lane = lax.broadcasted_iota(jnp.int32, (1, LANES), 1)
    head_mask = [lane < SB_HEAD_DIM, lane >= SB_HEAD_DIM]
    qm = [jnp.where(m, q, 0.0).astype(BF16) for m in head_mask]
    bias = [bias_ref[2 * g], bias_ref[2 * g + 1]]
    cum = [_cum_matrix(tk, LANES, 0, SB_HEAD_DIM), _cum_matrix(tk, LANES, SB_HEAD_DIM, LANES)]
    acc_ref[...] = jnp.zeros_like(acc_ref)
    car_ref[...] = jnp.zeros_like(car_ref)
    qpos = i * tq + lax.broadcasted_iota(jnp.int32, (tq, tk), 0)
    kiota = lax.broadcasted_iota(jnp.int32, (tq, tk), 1)
    blocks_per_step = tq // tk

    def one_block(k0, masked):
        kb = k_ref[0, pl.ds(k0, tk), :].astype(BF16)
        vb = v_ref[0, pl.ds(k0, tk), :]
        valid = ((k0 + kiota) < qpos) if masked else None
        ws, rs = [], None
        for h in range(2):
            z = lax.dot_general(qm[h], kb, (((1,), (1,)), ((), ())), preferred_element_type=F32) + bias[h]
            w, r = _sb_block(z, valid, cum[h])
            ws.append(w.astype(BF16))
            rs = r if rs is None else rs + r
        vm = jnp.concatenate([jnp.where(head_mask[h], vb, 0.0).astype(BF16) for h in range(2)], axis=0)
        o = jnp.dot(jnp.concatenate(ws, axis=1), vm, preferred_element_type=F32)
        return o, rs

    def step(first_block, masked):
        res = [one_block(pl.multiple_of(first_block * tk + r * tk, tk), masked)
               for r in range(blocks_per_step)]
        for o, rs in reversed(res):
            car = car_ref[...]
            acc_ref[...] += jnp.exp(car) * o
            car_ref[...] = car + rs

    step(i * blocks_per_step, True)

    def body(jj, _):
        step((i - 1 - jj) * blocks_per_step, False)
        return 0

    lax.fori_loop(0, i, body, 0)
    o_ref[0] = acc_ref[...]


def attn_prompt(proj3, sb_bias, tq=256, tk=128):
    b, t, _ = proj3.shape
    tq = min(tq, t)
    tk = min(tk, tq)
    n_pairs = SB_HEADS // 2
    qo = D_A // LANES
    body = functools.partial(_attn_prompt_body, tq=tq, tk=tk)
    return pl.pallas_call(
        body,
        grid=(b, n_pairs, t // tq),
        in_specs=[
            pl.BlockSpec(memory_space=pltpu.SMEM),
            pl.BlockSpec((1, tq, LANES), lambda bi, g, i: (bi, i, qo + g)),
            pl.BlockSpec((1, t, LANES), lambda bi, g, i: (bi, 0, qo + n_pairs + g)),
            pl.BlockSpec((1, t, LANES), lambda bi, g, i: (bi, 0, qo + 2 * n_pairs + g)),
        ],
        out_specs=pl.BlockSpec((1, tq, LANES), lambda bi, g, i: (bi, i, g)),
        scratch_shapes=[pltpu.VMEM((tq, LANES), F32), pltpu.VMEM((tq, LANES), F32)],
        out_shape=jax.ShapeDtypeStruct((b, t, D_B), F32),
        compiler_params=_cparams(("arbitrary", "arbitrary", "arbitrary")),
        name="attn_prompt",
    )(sb_bias, proj3, proj3, proj3)


def _attn_sample_body(pt_ref, q_ref, kn_ref, vn_ref, *rest, t_new, page, ppb):
    kp_refs, vp_refs = rest[:ppb], rest[ppb:2 * ppb]
    bias_ref, o_ref, qbd_ref, acc_ref, car_ref = rest[2 * ppb:]
    p = pl.program_id(1)
    n_steps = pl.num_programs(1)
    rows = SB_HEADS * t_new
    row_head = lax.broadcasted_iota(jnp.int32, (rows, D_B), 0) // t_new
    col_head = lax.broadcasted_iota(jnp.int32, (rows, D_B), 1) // SB_HEAD_DIM
    blockdiag = row_head == col_head
    cum = _cum_matrix(page, LANES, 0, LANES)

    def block(kb, vb, valid):
        z = lax.dot_general(qbd_ref[...], kb.astype(BF16), (((1,), (1,)), ((), ())),
                            preferred_element_type=F32) + bias_ref[...]
        w, rs = _sb_block(z, valid, cum)
        return jnp.dot(w.astype(BF16), vb.astype(BF16), preferred_element_type=F32), rs

    def accumulate(res):
        for o, rs in res:
            car = car_ref[...]
            e = jnp.exp(car)
            acc_ref[...] += jnp.concatenate([e] * (D_B // LANES), axis=1) * o
            car_ref[...] = car + rs

    @pl.when(p == 0)
    def _():
        scale = SB_HEAD_DIM ** -0.5
        q = q_ref[0] * scale
        qt = jnp.concatenate([q] * SB_HEADS, axis=0)
        qbd_ref[...] = jnp.where(blockdiag, qt, 0.0).astype(BF16)
        acc_ref[...] = jnp.zeros_like(acc_ref)
        car_ref[...] = jnp.zeros_like(car_ref)
        qi = lax.broadcasted_iota(jnp.int32, (rows, page), 0) % t_new
        ki = lax.broadcasted_iota(jnp.int32, (rows, page), 1)
        accumulate([block(kn_ref[0], vn_ref[0], ki < qi)])

    @pl.when(p > 0)
    def _():
        accumulate([block(kp_refs[r][0, 0], vp_refs[r][0, 0], None) for r in range(ppb)])

    @pl.when(p == n_steps - 1)
    def _():
        a = jnp.where(blockdiag, acc_ref[...], 0.0)
        out = a[0:t_new]
        for h in range(1, SB_HEADS):
            out = out + a[h * t_new:(h + 1) * t_new]
        o_ref[0] = out


def attn_sample(q, k_new, v_new, cache_k, cache_v, page_table, bias_rows):
    b, t_new, _ = q.shape
    page = cache_k.shape[2]
    n_pages = page_table.shape[1]
    rows = SB_HEADS * t_new
    ppb = math.gcd(n_pages, SAMPLE_PAGES_PER_STEP)

    def page_map(r):
        return lambda bi, p, pt: (0, pt[bi, n_pages - 1 - ((jnp.maximum(p, 1) - 1) * ppb + r)], 0, 0)

    page_specs = [pl.BlockSpec((1, 1, page, D_B), page_map(r)) for r in range(ppb)]
    grid_spec = pltpu.PrefetchScalarGridSpec(
        num_scalar_prefetch=1,
        grid=(b, n_pages // ppb + 1),
        in_specs=[
            pl.BlockSpec((1, t_new, D_B), lambda bi, p, pt: (bi, 0, 0)),
            pl.BlockSpec((1, page, D_B), lambda bi, p, pt: (bi, 0, 0)),
            pl.BlockSpec((1, page, D_B), lambda bi, p, pt: (bi, 0, 0)),
        ] + page_specs + page_specs + [
            pl.BlockSpec((rows, 1), lambda bi, p, pt: (0, 0)),
        ],
        out_specs=pl.BlockSpec((1, t_new, D_B), lambda bi, p, pt: (bi, 0, 0)),
        scratch_shapes=[pltpu.VMEM((rows, D_B), BF16), pltpu.VMEM((rows, D_B), F32),
                        pltpu.VMEM((rows, LANES), F32)],
    )
    body = functools.partial(_attn_sample_body, t_new=t_new, page=page, ppb=ppb)
    return pl.pallas_call(
        body,
        grid_spec=grid_spec,
        out_shape=jax.ShapeDtypeStruct((b, t_new, D_B), F32),
        compiler_params=_cparams(("arbitrary", "arbitrary")),
        name="attn_sample",
    )(page_table, q, k_new, v_new, *([cache_k] * ppb), *([cache_v] * ppb), bias_rows)


CONV_HALO = 8
CONV_CB = 1024


def _conv_body(cur_ref, halo_ref, buf_ref, w_ref, b_ref, y_ref, nb_ref, win_ref, *, ch):
    c = pl.program_id(2)
    tail = jnp.where(c == 0, buf_ref[0], halo_ref[0])
    win_ref[0:CONV_HALO, :] = tail
    win_ref[CONV_HALO:CONV_HALO + ch, :] = cur_ref[0]
    w = w_ref[...]
    y = b_ref[...]
    for i in range(SSD_CONV):
        off = CONV_HALO - (SSD_CONV - 1) + i
        y = y + win_ref[off:off + ch, :] * w[i:i + 1, :]
    y_ref[0] = y * jax.nn.sigmoid(y)
    nb_ref[0] = win_ref[ch:ch + CONV_HALO, :]


def conv_silu(proj3, buf8, conv_w, conv_b):
    b, t, _ = proj3.shape
    ch = min(128, t)
    nc = t // ch
    col0 = D_INNER // CONV_CB
    ncb = CONV_DIM // CONV_CB
    hb = ch // CONV_HALO
    body = functools.partial(_conv_body, ch=ch)
    return pl.pallas_call(
        body,
        grid=(b, ncb, nc),
        in_specs=[
            pl.BlockSpec((1, ch, CONV_CB), lambda i, j, c: (i, c, col0 + j)),
            pl.BlockSpec((1, CONV_HALO, CONV_CB), lambda i, j, c: (i, jnp.maximum(c * hb - 1, 0), col0 + j)),
            pl.BlockSpec((1, CONV_HALO, CONV_CB), lambda i, j, c: (i, 0, j)),
            pl.BlockSpec((SSD_CONV, CONV_CB), lambda i, j, c: (0, j)),
            pl.BlockSpec((1, CONV_CB), lambda i, j, c: (0, j)),
        ],
        out_specs=[
            pl.BlockSpec((1, ch, CONV_CB), lambda i, j, c: (i, c, j)),
            pl.BlockSpec((1, CONV_HALO, CONV_CB), lambda i, j, c: (i, 0, j)),
        ],
        out_shape=[
            jax.ShapeDtypeStruct((b, t, CONV_DIM), F32),
            jax.ShapeDtypeStruct((b, CONV_HALO, CONV_DIM), F32),
        ],
        scratch_shapes=[pltpu.VMEM((CONV_HALO + ch, CONV_CB), F32)],
        compiler_params=_cparams(("arbitrary", "arbitrary", "arbitrary")),
        name="conv",
    )(proj3, proj3, buf8, conv_w, conv_b)


def _ssd_body(x_ref, b_ref, c_ref, z_ref, dt_ref, dtb_ref, alog_ref, d_ref, ng_ref, h0_ref,
              y_ref, hout_ref, st_ref, *, cl, t_valid):
    c = pl.program_id(1)
    nc = pl.num_programs(1)
    n_pairs = SSD_HEADS // 2
    pairs_per_group = n_pairs // SSD_GROUPS

    @pl.when(c == 0)
    def _():
        for m in range(n_pairs):
            st_ref[m] = h0_ref[0, m].T

    row = lax.broadcasted_iota(jnp.int32, (cl, LANES), 0)
    lane = lax.broadcasted_iota(jnp.int32, (cl, LANES), 1)
    dt = jax.nn.softplus(dt_ref[0] + dtb_ref[...])
    dt = jnp.where((c * cl + row < t_valid) & (lane < SSD_HEADS), dt, 0.0)
    a = -jnp.exp(alog_ref[...])
    da = dt * a
    tri_r = lax.broadcasted_iota(jnp.int32, (cl, cl), 0)
    tri_c = lax.broadcasted_iota(jnp.int32, (cl, cl), 1)
    causal = tri_r >= tri_c
    ltri = jnp.where(causal, 1.0, 0.0).astype(BF16)
    acs = _dot_exact_rhs(ltri, da)
    acs_t = acs.T
    er = lax.broadcasted_iota(jnp.int32, (LANES, D_INNER), 0)
    ec = lax.broadcasted_iota(jnp.int32, (LANES, D_INNER), 1) // SSD_HEAD_DIM
    expand = jnp.where(er == ec, 1.0, 0.0).astype(BF16)
    dt_full = _dot_exact_lhs(dt, expand)
    acs_full = _dot_exact_lhs(acs, expand)
    last_full = acs_full[cl - 1:cl, :]
    xall = x_ref[0]
    xdt = xall * dt_full
    xdt_end = (xdt * jnp.exp(last_full - acs_full)).astype(BF16)
    xdt_b = xdt.astype(BF16)
    e_acs = jnp.exp(acs_full)
    chunk_decay = jnp.exp(last_full)
    lane_lo = lax.broadcasted_iota(jnp.int32, (1, LANES), 1) < SSD_HEAD_DIM
    ys = []
    for g in range(SSD_GROUPS):
        bg = b_ref[0][:, g * SSD_STATE:(g + 1) * SSD_STATE]
        cg = c_ref[0][:, g * SSD_STATE:(g + 1) * SSD_STATE].astype(BF16)
        cb = lax.dot_general(cg, bg.astype(BF16), (((1,), (1,)), ((), ())), preferred_element_type=F32)
        bg_t = bg.T.astype(BF16)
        for mm in range(pairs_per_group):
            m = g * pairs_per_group + mm
            sl = slice(m * LANES, (m + 1) * LANES)
            yd = None
            for hh in range(2):
                h = 2 * m + hh
                seg = acs[:, h:h + 1] - acs_t[h:h + 1, :]
                decay = jnp.exp(jnp.where(causal, seg, NEG_INF))
                sc = (cb * decay).astype(BF16)
                xm = jnp.where(lane_lo if hh == 0 else ~lane_lo, xdt_b[:, sl], jnp.zeros((), BF16))
                d = jnp.dot(sc, xm, preferred_element_type=F32)
                yd = d if yd is None else yd + d
            st = st_ref[m]
            yo = jnp.dot(cg, st.astype(BF16), preferred_element_type=F32) * e_acs[:, sl]
            cs = jnp.dot(bg_t, xdt_end[:, sl], preferred_element_type=F32)
            st_ref[m] = st * chunk_decay[:, sl] + cs
            ys.append(yd + yo)
    y = jnp.concatenate(ys, axis=1) + xall * d_ref[...]
    zz = z_ref[0]
    y = y * (zz * jax.nn.sigmoid(zz))
    gw = D_INNER // SSD_GROUPS
    outs = []
    for g in range(SSD_GROUPS):
        s = y[:, g * gw:(g + 1) * gw]
        r = lax.rsqrt(jnp.mean(s * s, axis=-1, keepdims=True) + EPS)
        outs.append(s * r)
    y_ref[0] = jnp.concatenate(outs, axis=1) * ng_ref[...]

    @pl.when(c == nc - 1)
    def _():
        for m in range(n_pairs):
            hout_ref[0, m] = st_ref[m].T


def ssd_scan(xact, proj3, dt_bias, a_log, d_full, norm_g, h0, t_valid):
    b, t, _ = xact.shape
    cl = SSD_CHUNK
    nc = t // cl
    gn = SSD_GROUPS * SSD_STATE
    body = functools.partial(_ssd_body, cl=cl, t_valid=t_valid)
    n_pairs = SSD_HEADS // 2
    return pl.pallas_call(
        body,
        grid=(b, nc),
        in_specs=[
            pl.BlockSpec((1, cl, D_INNER), lambda i, c: (i, c, 0)),
            pl.BlockSpec((1, cl, gn), lambda i, c: (i, c, D_INNER // gn)),
            pl.BlockSpec((1, cl, gn), lambda i, c: (i, c, D_INNER // gn + 1)),
            pl.BlockSpec((1, cl, D_INNER), lambda i, c: (i, c, 0)),
            pl.BlockSpec((1, cl, LANES), lambda i, c: (i, c, (D_INNER + CONV_DIM) // LANES)),
            pl.BlockSpec((1, LANES), lambda i, c: (0, 0)),
            pl.BlockSpec((1, LANES), lambda i, c: (0, 0)),
            pl.BlockSpec((1, D_INNER), lambda i, c: (0, 0)),
            pl.BlockSpec((1, D_INNER), lambda i, c: (0, 0)),
            pl.BlockSpec((1, n_pairs, LANES, SSD_STATE), lambda i, c: (i, 0, 0, 0)),
        ],
        out_specs=[
            pl.BlockSpec((1, cl, D_INNER), lambda i, c: (i, c, 0)),
            pl.BlockSpec((1, n_pairs, LANES, SSD_STATE), lambda i, c: (i, 0, 0, 0)),
        ],
        out_shape=[
            jax.ShapeDtypeStruct((b, t, D_INNER), F32),
            jax.ShapeDtypeStruct((b, n_pairs, LANES, SSD_STATE), F32),
        ],
        scratch_shapes=[pltpu.VMEM((n_pairs, SSD_STATE, LANES), F32)],
        compiler_params=_cparams(("arbitrary", "arbitrary")),
        name="ssd_scan",
    )(xact, xact, xact, proj3, proj3, dt_bias, a_log, d_full, norm_g, h0)


def _top_rows(v, k):
    n = v.shape[0]
    rows = lax.broadcasted_iota(jnp.int32, v.shape, 0)
    outs = []
    for _ in range(k):
        m = jnp.max(v, axis=0, keepdims=True)
        outs.append(m)
        first = jnp.min(jnp.where(v == m, rows, n), axis=0, keepdims=True)
        v = jnp.where(rows == first, NEG_INF, v)
    return jnp.concatenate(outs, axis=0)


def _peer_body(x_ref, g_ref, sh_ref, sc_ref, gt_ref, wq_ref, k1_ref, k2_ref, u_ref, v_ref, og_ref,
               o_ref, ht_ref, s1_ref, s2_ref, f1_ref, f2_ref, tau_ref, acc_ref, *, tt, eb, final_norm):
    j = pl.program_id(1)
    nj = pl.num_programs(1)
    ncol = tt // LANES

    @pl.when(j == 0)
    def _():
        h = _modulate(x_ref[...], g_ref[...], sh_ref[0], sc_ref[0])
        ht_ref[...] = h.T.astype(BF16)
        acc_ref[...] = jnp.zeros_like(acc_ref)

        def per_head(hd, _):
            r0 = pl.multiple_of(hd * D_KEY, D_KEY)
            qh = jnp.dot(wq_ref[pl.ds(r0, D_KEY), :], ht_ref[...], preferred_element_type=F32)
            s1 = jnp.dot(k1_ref[hd], qh[:D_HALF].astype(BF16), preferred_element_type=F32)
            s2 = jnp.dot(k2_ref[hd], qh[D_HALF:].astype(BF16), preferred_element_type=F32)
            v1 = _top_rows(s1, PEER_TOPK)
            v2 = _top_rows(s2, PEER_TOPK)
            cands = []
            for a_ in range(PEER_TOPK):
                nb = PEER_TOPK // (a_ + 1)
                cands.append(v1[a_:a_ + 1] + v2[0:nb])
            cand = jnp.concatenate(cands, axis=0)
            best = _top_rows(cand, PEER_TOPK)
            zsum = jnp.sum(jnp.exp(best - best[0:1]), axis=0, keepdims=True)
            f1 = jnp.exp(s1 - v1[0:1]) / zsum
            f2 = jnp.exp(s2 - v2[0:1])
            tau_ref[hd] = best[PEER_TOPK - 1:PEER_TOPK]
            for cc in range(ncol):
                cs = slice(cc * LANES, (cc + 1) * LANES)
                s1_ref[hd, cc] = s1[:, cs]
                s2_ref[hd, cc] = s2[:, cs]
                f1_ref[hd, cc] = f1[:, cs]
                f2_ref[hd, cc] = f2[:, cs]
            return 0

        lax.fori_loop(0, PEER_HEADS, per_head, 0)

    a = jnp.dot(u_ref[...], ht_ref[...], preferred_element_type=F32)
    act = 0.5 * a * (1.0 + lax.erf(a * (2.0 ** -0.5)))
    n_i1 = eb // N_KEYS
    blocks = []
    for ii in range(n_i1):
        i1 = j * n_i1 + ii
        cols = []
        for cc in range(ncol):
            w = jnp.zeros((N_KEYS, LANES), F32)
            for hd in range(PEER_HEADS):
                s1r = s1_ref[hd, cc, pl.ds(i1, 1), :]
                f1r = f1_ref[hd, cc, pl.ds(i1, 1), :]
                tau = tau_ref[hd, :, cc * LANES:(cc + 1) * LANES]
                sel = (s1r + s2_ref[hd, cc]) >= tau
                w = w + jnp.where(sel, f1r * f2_ref[hd, cc], 0.0)
            cols.append(w)
        blocks.append(jnp.concatenate(cols, axis=1))
    wfull = jnp.concatenate(blocks, axis=0)
    p = (wfull * act).astype(BF16)
    acc_ref[...] += lax.dot_general(p, v_ref[...], (((0,), (0,)), ((), ())),
                                    preferred_element_type=F32)

    @pl.when(j == nj - 1)
    def _():
        y = x_ref[...] + gt_ref[0] * acc_ref[...]
        if final_norm:
            r = lax.rsqrt(jnp.mean(y * y, axis=-1, keepdims=True) + EPS)
            y = (y * r) * og_ref[...]
        o_ref[...] = y


def peer_layer(x, g, sh, sc, gt, wq_t, k1, k2, u, v, out_g, tt, eb=512, final_norm=False):
    n = x.shape[0]
    nt = n // tt
    ncol = tt // LANES
    nj = N_EXPERTS // eb
    body = functools.partial(_peer_body, tt=tt, eb=eb, final_norm=final_norm)

    def grp(a):
        gg, r, c = a.shape
        tpg = nt // gg
        return pl.BlockSpec((1, r, c), lambda i, j: (i // tpg, 0, 0))

    return pl.pallas_call(
        body,
        grid=(nt, nj),
        in_specs=[
            pl.BlockSpec((tt, D_MODEL), lambda i, j: (i, 0)),
            pl.BlockSpec((1, D_MODEL), lambda i, j: (0, 0)),
            grp(sh), grp(sc), grp(gt),
            pl.BlockSpec(wq_t.shape, lambda i, j: (0, 0)),
            pl.BlockSpec(k1.shape, lambda i, j: (0, 0, 0)),
            pl.BlockSpec(k2.shape, lambda i, j: (0, 0, 0)),
            pl.BlockSpec((eb, D_MODEL), lambda i, j: (j, 0)),
            pl.BlockSpec((eb, D_MODEL), lambda i, j: (j, 0)),
            pl.BlockSpec((1, D_MODEL), lambda i, j: (0, 0)),
        ],
        out_specs=pl.BlockSpec((tt, D_MODEL), lambda i, j: (i, 0)),
        out_shape=jax.ShapeDtypeStruct((n, D_MODEL), F32),
        scratch_shapes=[
            pltpu.VMEM((D_MODEL, tt), BF16),
            pltpu.VMEM((PEER_HEADS, ncol, N_KEYS, LANES), F32),
            pltpu.VMEM((PEER_HEADS, ncol, N_KEYS, LANES), F32),
            pltpu.VMEM((PEER_HEADS, ncol, N_KEYS, LANES), F32),
            pltpu.VMEM((PEER_HEADS, ncol, N_KEYS, LANES), F32),
            pltpu.VMEM((PEER_HEADS, 1, tt), F32),
            pltpu.VMEM((tt, D_MODEL), F32),
        ],
        compiler_params=_cparams(("arbitrary", "arbitrary")),
        name="peer",
    )(x, g, sh, sc, gt, wq_t, k1, k2, u, v, out_g)


def _row_groups(a, bsz, t, tm):
    if t % tm == 0:
        return a[:, None, :]
    assert tm % t == 0
    per_row = jnp.repeat(a, t, axis=0)
    return per_row.reshape(bsz * t // tm, tm, a.shape[1])


def _trunk(x, c_mod, start_pos, past, pool_buf, conv_buf, ssm_state, wts, tm, tt):
    bsz, t, _ = x.shape
    n = bsz * t
    x2 = x.reshape(n, D_MODEL)
    rg = lambda a: _row_groups(a, bsz, t, tm)
    rgp = lambda a: _row_groups(a, bsz, t, tt)

    sh1, sc1, g1, sh2, sc2, g2 = c_mod[0]
    proj = matmul([x2], [wts["ab_w_in"]], tm, pro=(wts["norm_mix_g"][0], rg(sh1), rg(sc1)), name="ab_in")
    proj3 = proj.reshape(bsz, t, -1)
    k_new = proj3[..., D_A + D_B:D_A + 2 * D_B]
    v_new = proj3[..., D_A + 2 * D_B:]
    buf16 = jnp.pad(pool_buf, ((0, 0), (POOL_HALO - POOL_BUF, 0), (0, 0)))
    ya, nb16 = pool_mixer(proj3, buf16, wts["pool_wbd"], wts["pool_scale"], start_pos)
    new_pool = nb16[:, POOL_HALO - POOL_BUF:]
    if past is None:
        yb = attn_prompt(proj3, wts["sb_bias"])
    else:
        cache_k, cache_v, page_table = past
        page = cache_k.shape[2]
        q = proj3[..., D_A:D_A + D_B]
        padr = ((0, 0), (0, page - t), (0, 0))
        bias_rows = jnp.repeat(wts["sb_bias"], t)[:, None]
        yb = attn_sample(q, jnp.pad(k_new, padr), jnp.pad(v_new, padr), cache_k, cache_v, page_table, bias_rows)
    x2 = matmul([ya.reshape(n, D_A), yb.reshape(n, D_B)], [wts["ab_w_out_a"], wts["ab_w_out_b"]], tm,
                res=(x2, rg(g1)), name="ab_out")
    x2 = peer_layer(x2, wts["norm_ffn_g"][0], rgp(sh2), rgp(sc2), rgp(g2), wts["peer_wq_t"][0],
                    wts["peer_k1"][0], wts["peer_k2"][0], wts["peer_u"][0], wts["peer_v"][0],
                    wts["norm_out_g"], tt)

    sh1, sc1, g1, sh2, sc2, g2 = c_mod[1]
    proj = matmul([x2], [wts["ssd_w_in"]], tm, pro=(wts["norm_mix_g"][1], rg(sh1), rg(sc1)), name="ssd_in")
    proj3 = proj.reshape(bsz, t, -1)
    buf8 = jnp.pad(conv_buf, ((0, 0), (CONV_HALO - (SSD_CONV - 1), 0), (0, 0)))
    xact, nc8 = conv_silu(proj3, buf8, wts["ssd_conv_w"], wts["ssd_conv_b"])
    new_conv = nc8[:, CONV_HALO - (SSD_CONV - 1):]
    tp = -(-t // SSD_CHUNK) * SSD_CHUNK
    if tp != t:
        padt = ((0, 0), (0, tp - t), (0, 0))
        xact = jnp.pad(xact, padt)
        proj3 = jnp.pad(proj3, padt)
    h0 = ssm_state.reshape(bsz, SSD_HEADS // 2, LANES, SSD_STATE)
    ynorm, hlast = ssd_scan(xact, proj3, wts["ssd_dt_bias"], wts["ssd_a_log"], wts["ssd_d_full"],
                            wts["ssd_norm_g"], h0, t)
    new_ssm = hlast.reshape(bsz, SSD_HEADS, SSD_HEAD_DIM, SSD_STATE)
    ynorm = ynorm[:, :t].reshape(n, D_INNER)
    x2 = matmul([ynorm], [wts["ssd_w_out"]], tm, res=(x2, rg(g1)), name="ssd_out")
    x2 = peer_layer(x2, wts["norm_ffn_g"][1], rgp(sh2), rgp(sc2), rgp(g2), wts["peer_wq_t"][1],
                    wts["peer_k1"][1], wts["peer_k2"][1], wts["peer_u"][1], wts["peer_v"][1],
                    wts["norm_out_g"], tt, final_norm=True)
    return (x2.reshape(bsz, t, D_MODEL), k_new.reshape(bsz, t, SB_HEADS, SB_HEAD_DIM)[None],
            v_new.reshape(bsz, t, SB_HEADS, SB_HEAD_DIM)[None], new_pool[None], new_conv[None], new_ssm[None])


def kernel(x_prompt, x_sample, cache_sb_k, cache_sb_v, state_pool, state_conv, state_ssm, page_table,
           c_prompt, c_sample, ada_w, ada_b, norm_mix_g, norm_ffn_g, norm_out_g,
           ab_w_in, ab_w_out, pool_w, pool_scale, sb_bias,
           ssd_w_in, ssd_conv_w, ssd_conv_b, ssd_dt_bias, ssd_a_log, ssd_d, ssd_norm_g, ssd_w_out,
           peer_wq, peer_k1, peer_k2, peer_u, peer_v):
    bsz, seq, _ = x_prompt.shape
    dbsz, dseq, _ = x_sample.shape
    depth = ada_w.shape[0]
    assert depth == 2 and ab_w_in.shape[0] == 1 and ssd_w_in.shape[0] == 1

    ssd_pad = (-D_SSD_IN) % LANES
    pad_lanes = lambda a: jnp.pad(a, ((0, 0), (0, LANES - a.shape[1])))
    wts = {
        "norm_mix_g": [norm_mix_g[l][None] for l in range(depth)],
        "norm_ffn_g": [norm_ffn_g[l][None] for l in range(depth)],
        "norm_out_g": norm_out_g[None],
        "ab_w_in": ab_w_in[0].astype(BF16),
        "ab_w_out_a": ab_w_out[0, :D_A].astype(BF16),
        "ab_w_out_b": ab_w_out[0, D_A:].astype(BF16),
        "pool_wbd": jax.scipy.linalg.block_diag(*[pool_w[0, g] for g in range(len(POOL_WINDOWS))]).astype(BF16),
        "pool_scale": pool_scale[0][None],
        "sb_bias": sb_bias[0],
        "ssd_w_in": jnp.pad(ssd_w_in[0], ((0, 0), (0, ssd_pad))).astype(BF16),
        "ssd_conv_w": ssd_conv_w[0],
        "ssd_conv_b": ssd_conv_b[0][None],
        "ssd_dt_bias": pad_lanes(ssd_dt_bias[0][None]),
        "ssd_a_log": pad_lanes(ssd_a_log[0][None]),
        "ssd_d_full": jnp.repeat(ssd_d[0], SSD_HEAD_DIM)[None],
        "ssd_norm_g": ssd_norm_g[0][None],
        "ssd_w_out": ssd_w_out[0].astype(BF16),
        "peer_wq_t": [peer_wq[l].T.astype(BF16) for l in range(depth)],
        "peer_k1": [peer_k1[l].astype(BF16) for l in range(depth)],
        "peer_k2": [peer_k2[l].astype(BF16) for l in range(depth)],
        "peer_u": [peer_u[l].astype(BF16) for l in range(depth)],
        "peer_v": [peer_v[l].astype(BF16) for l in range(depth)],
    }

    c_all = jnp.concatenate([c_prompt, c_sample], axis=0)
    mods_p, mods_s = [], []
    for l in range(depth):
        mod = matmul([c_all], [ada_w[l].astype(BF16)], c_all.shape[0], bias=ada_b[l][None], name="ada")
        parts = jnp.split(mod, 6, axis=-1)
        mods_p.append([p[:bsz] for p in parts])
        mods_s.append([p[bsz:] for p in parts])

    zeros = lambda shape: jnp.zeros(shape, F32)
    out_p = _trunk(x_prompt, mods_p, 0, None, zeros((bsz, POOL_BUF, D_A)), zeros((bsz, SSD_CONV - 1, CONV_DIM)),
                   zeros((bsz, SSD_HEADS, SSD_HEAD_DIM, SSD_STATE)), wts, tm=256, tt=512)
    n_pool, page = cache_sb_k.shape[1], cache_sb_k.shape[2]
    past = (cache_sb_k.reshape(1, n_pool, page, D_B), cache_sb_v.reshape(1, n_pool, page, D_B), page_table)
    past_len = page_table.shape[1] * page
    out_s = _trunk(x_sample, mods_s, past_len, past, state_pool[0], state_conv[0], state_ssm[0], wts,
                   tm=dbsz * dseq, tt=dbsz * dseq)
    y_p, k_p, v_p, pool_p, conv_p, ssm_p = out_p
    y_s, k_s, v_s, pool_s, conv_s, ssm_s = out_s
    return (y_p, y_s, k_p, v_p, k_s, v_s, pool_p, pool_s, conv_p, conv_s, ssm_p, ssm_s)
```

```python
import functools
import math

import jax
import jax.numpy as jnp
from jax import lax
from jax.experimental import pallas as pl
from jax.experimental.pallas import tpu as pltpu

F32 = jnp.float32
BF16 = jnp.bfloat16

D_MODEL = 1024
POOL_WINDOWS = (2, 4, 8, 16)
D_A = 256
POOL_GC = 64
POOL_BUF = 15
SB_HEAD_DIM = 64
D_B = 768
SB_HEADS = 12
D_INNER = 2048
SSD_HEAD_DIM = 64
SSD_HEADS = 32
SSD_GROUPS = 4
SSD_STATE = 128
SSD_CONV = 4
SSD_CHUNK = 128
CONV_DIM = 3072
D_SSD_IN = D_INNER + CONV_DIM + SSD_HEADS
PEER_HEADS = 8
N_KEYS = 128
N_EXPERTS = N_KEYS * N_KEYS
PEER_TOPK = 16
D_KEY = 256
D_HALF = 128
EPS = 1e-6

LANES = 128
SUBLANES = 8
VMEM_LIMIT = 56 * 1024 * 1024

NEG_INF = float("-inf")


def _cparams(sem):
    return pltpu.CompilerParams(dimension_semantics=sem, vmem_limit_bytes=VMEM_LIMIT)


def _modulate(x, g, shift, scale):
    r = lax.rsqrt(jnp.mean(x * x, axis=-1, keepdims=True) + EPS)
    return (x * r) * g * (1.0 + scale) + shift


def _split3(x):
    hi = x.astype(BF16)
    r1 = x - hi.astype(F32)
    mid = r1.astype(BF16)
    lo = (r1 - mid.astype(F32)).astype(BF16)
    return hi, mid, lo


def _dot_exact_rhs(a_bf16, x):
    hi, mid, lo = _split3(x)
    return (jnp.dot(a_bf16, hi, preferred_element_type=F32)
            + jnp.dot(a_bf16, mid, preferred_element_type=F32)
            + jnp.dot(a_bf16, lo, preferred_element_type=F32))


def _dot_exact_lhs(x, a_bf16):
    hi, mid, lo = _split3(x)
    return (jnp.dot(hi, a_bf16, preferred_element_type=F32)
            + jnp.dot(mid, a_bf16, preferred_element_type=F32)
            + jnp.dot(lo, a_bf16, preferred_element_type=F32))


def _mm_body(*refs, n_x, has_pro, has_bias, has_res):
    it = iter(refs)
    xs = [next(it) for _ in range(n_x)]
    ws = [next(it) for _ in range(n_x)]
    if has_pro:
        g_ref, sh_ref, sc_ref = next(it), next(it), next(it)
    if has_bias:
        b_ref = next(it)
    if has_res:
        r_ref, gt_ref = next(it), next(it)
    o_ref = next(it)
    acc = None
    for x_ref, w_ref in zip(xs, ws):
        x = x_ref[...]
        if has_pro:
            x = _modulate(x, g_ref[...], sh_ref[0], sc_ref[0])
        d = jnp.dot(x.astype(BF16), w_ref[...], preferred_element_type=F32)
        acc = d if acc is None else acc + d
    if has_bias:
        acc = acc + b_ref[...]
    if has_res:
        acc = r_ref[...] + gt_ref[0] * acc
    o_ref[...] = acc


def _group_spec(a, n_tiles):
    g, r, c = a.shape
    tiles_per_group = n_tiles // g
    return pl.BlockSpec((1, r, c), lambda i: (i // tiles_per_group, 0, 0))


def matmul(xs, ws, tm, pro=None, bias=None, res=None, name="mm"):
    m = xs[0].shape[0]
    n = ws[0].shape[1]
    nt = m // tm
    assert nt * tm == m
    in_specs = [pl.BlockSpec((tm, x.shape[1]), lambda i: (i, 0)) for x in xs]
    in_specs += [pl.BlockSpec(w.shape, lambda i: (0, 0)) for w in ws]
    args = list(xs) + list(ws)
    if pro is not None:
        g, sh, sc = pro
        in_specs += [pl.BlockSpec(g.shape, lambda i: (0, 0)), _group_spec(sh, nt), _group_spec(sc, nt)]
        args += [g, sh, sc]
    if bias is not None:
        in_specs.append(pl.BlockSpec(bias.shape, lambda i: (0, 0)))
        args.append(bias)
    if res is not None:
        r, gt = res
        in_specs += [pl.BlockSpec((tm, n), lambda i: (i, 0)), _group_spec(gt, nt)]
        args += [r, gt]
    body = functools.partial(_mm_body, n_x=len(xs), has_pro=pro is not None,
                             has_bias=bias is not None, has_res=res is not None)
    return pl.pallas_call(
        body,
        grid=(nt,),
        in_specs=in_specs,
        out_specs=pl.BlockSpec((tm, n), lambda i: (i, 0)),
        out_shape=jax.ShapeDtypeStruct((m, n), F32),
        compiler_params=_cparams(("arbitrary",)),
        name=name,
    )(*args)


POOL_HALO = 16


def _pool_body(cur_ref, prev_ref, buf_ref, wbd_ref, ps_ref, y_ref, nb_ref, win_ref, *, ch, start_pos):
    c = pl.program_id(1)
    cur = cur_ref[0]
    if ch >= POOL_HALO:
        tail = jnp.where(c == 0, buf_ref[0], prev_ref[0][ch - POOL_HALO:ch])
    else:
        tail = buf_ref[0]
    win_ref[0:POOL_HALO, :] = tail
    win_ref[POOL_HALO:POOL_HALO + ch, :] = cur

    def shifted(k):
        return win_ref[POOL_HALO - k:POOL_HALO - k + ch, :]

    pos = start_pos + c * ch + lax.broadcasted_iota(jnp.int32, (ch, 1), 0)
    grp = lax.broadcasted_iota(jnp.int32, (1, D_A), 1) // POOL_GC
    s = cur
    mean = jnp.zeros_like(cur)
    k = 1
    for g, w in enumerate(POOL_WINDOWS):
        while k < w:
            s = s + shifted(k)
            k += 1
        cnt = jnp.minimum(pos + 1, w).astype(F32)
        mean = jnp.where(grp == g, s / cnt, mean)
    diff = mean - cur
    y = jnp.dot(diff.astype(BF16), wbd_ref[...], preferred_element_type=F32) * ps_ref[...]
    y_ref[0] = y
    nb_ref[0] = win_ref[ch:ch + POOL_HALO, :]


def pool_mixer(proj3, buf16, wbd, pscale, start_pos):
    b, t, _ = proj3.shape
    ch = min(128, t)
    nc = t // ch
    body = functools.partial(_pool_body, ch=ch, start_pos=start_pos)
    return pl.pallas_call(
        body,
        grid=(b, nc),
        in_specs=[
            pl.BlockSpec((1, ch, D_A), lambda i, c: (i, c, 0)),
            pl.BlockSpec((1, ch, D_A), lambda i, c: (i, jnp.maximum(c - 1, 0), 0)),
            pl.BlockSpec((1, POOL_HALO, D_A), lambda i, c: (i, 0, 0)),
            pl.BlockSpec((D_A, D_A), lambda i, c: (0, 0)),
            pl.BlockSpec((1, D_A), lambda i, c: (0, 0)),
        ],
        out_specs=[
            pl.BlockSpec((1, ch, D_A), lambda i, c: (i, c, 0)),
            pl.BlockSpec((1, POOL_HALO, D_A), lambda i, c: (i, 0, 0)),
        ],
        out_shape=[
            jax.ShapeDtypeStruct((b, t, D_A), F32),
            jax.ShapeDtypeStruct((b, POOL_HALO, D_A), F32),
        ],
        scratch_shapes=[pltpu.VMEM((POOL_HALO + ch, D_A), F32)],
        compiler_params=_cparams(("arbitrary", "arbitrary")),
        name="pool",
    )(proj3, proj3, buf16, wbd, pscale)


SAMPLE_PAGES_PER_STEP = 4


def _cum_matrix(tk, n_ones, ones_lo, ones_hi):
    r = lax.broadcasted_iota(jnp.int32, (tk, tk + n_ones), 0)
    c = lax.broadcasted_iota(jnp.int32, (tk, tk + n_ones), 1)
    ones = (c >= tk + ones_lo) & (c < tk + ones_hi)
    return jnp.where(ones | ((c < tk) & (r > c)), 1.0, 0.0).astype(BF16)


def _sb_block(z, valid, cum):
    tk = z.shape[1]
    l1p = jnp.log1p(jnp.exp(-jnp.abs(z)))
    log_beta = jnp.minimum(z, 0.0) - l1p
    log_keep = jnp.minimum(-z, 0.0) - l1p
    if valid is not None:
        log_keep = jnp.where(valid, log_keep, 0.0)
    hi = log_keep.astype(BF16)
    lo = (log_keep - hi.astype(F32)).astype(BF16)
    cs = (jnp.dot(hi, cum, preferred_element_type=F32)
          + jnp.dot(lo, cum, preferred_element_type=F32))
    w = jnp.exp(log_beta + cs[:, :tk])
    if valid is not None:
        w = jnp.where(valid, w, 0.0)
    return w, cs[:, tk:]


def _attn_prompt_body(bias_ref, q_ref, k_ref, v_ref, o_ref, acc_ref, car_ref, *, tq, tk):
    g = pl.program_id(1)
    i = pl.program_id(2)
    scale = SB_HEAD_DIM ** -0.5
    q = q_ref[0] * scale
    lane = lax.broadcasted_iota(jnp.int32, (1, LANES), 1)
    head_mask = [lane < SB_HEAD_DIM, lane >= SB_HEAD_DIM]
    qm = [jnp.where(m, q, 0.0).astype(BF16) for m in head_mask]
    bias = [bias_ref[2 * g], bias_ref[2 * g + 1]]
    cum = [_cum_matrix(tk, LANES, 0, SB_HEAD_DIM), _cum_matrix(tk, LANES, SB_HEAD_DIM, LANES)]
    acc_ref[...] = jnp.zeros_like(acc_ref)
    car_ref[...] = jnp.zeros_like(car_ref)
    qpos = i * tq + lax.broadcasted_iota(jnp.int32, (tq, tk), 0)
    kiota = lax.broadcasted_iota(jnp.int32, (tq, tk), 1)
    blocks_per_step = tq // tk

    def one_block(k0, masked):
        kb = k_ref[0, pl.ds(k0, tk), :].astype(BF16)
        vb = v_ref[0, pl.ds(k0, tk), :]
        valid = ((k0 + kiota) < qpos) if masked else None
        ws, rs = [], None
        for h in range(2):
            z = lax.dot_general(qm[h], kb, (((1,), (1,)), ((), ())), preferred_element_type=F32) + bias[h]
            w, r = _sb_block(z, valid, cum[h])
            ws.append(w.astype(BF16))
            rs = r if rs is None else rs + r
        vm = jnp.concatenate([jnp.where(head_mask[h], vb, 0.0).astype(BF16) for h in range(2)], axis=0)
        o = jnp.dot(jnp.concatenate(ws, axis=1), vm, preferred_element_type=F32)
        return o, rs

    def step(first_block, masked):
        res = [one_block(pl.multiple_of(first_block * tk + r * tk, tk), masked)
               for r in range(blocks_per_step)]
        for o, rs in reversed(res):
            car = car_ref[...]
            acc_ref[...] += jnp.exp(car) * o
            car_ref[...] = car + rs

    step(i * blocks_per_step, True)

    def body(jj, _):
        step((i - 1 - jj) * blocks_per_step, False)
        return 0

    lax.fori_loop(0, i, body, 0)
    o_ref[0] = acc_ref[...]


def attn_prompt(proj3, sb_bias, tq=256, tk=128):
    b, t, _ = proj3.shape
    tq = min(tq, t)
    tk = min(tk, tq)
    n_pairs = SB_HEADS // 2
    qo = D_A // LANES
    body = functools.partial(_attn_prompt_body, tq=tq, tk=tk)
    return pl.pallas_call(
        body,
        grid=(b, n_pairs, t // tq),
        in_specs=[
            pl.BlockSpec(memory_space=pltpu.SMEM),
            pl.BlockSpec((1, tq, LANES), lambda bi, g, i: (bi, i, qo + g)),
            pl.BlockSpec((1, t, LANES), lambda bi, g, i: (bi, 0, qo + n_pairs + g)),
            pl.BlockSpec((1, t, LANES), lambda bi, g, i: (bi, 0, qo + 2 * n_pairs + g)),
        ],
        out_specs=pl.BlockSpec((1, tq, LANES), lambda bi, g, i: (bi, i, g)),
        scratch_shapes=[pltpu.VMEM((tq, LANES), F32), pltpu.VMEM((tq, LANES), F32)],
        out_shape=jax.ShapeDtypeStruct((b, t, D_B), F32),
        compiler_params=_cparams(("arbitrary", "arbitrary", "arbitrary")),
        name="attn_prompt",
    )(sb_bias, proj3, proj3, proj3)


def _attn_sample_body(pt_ref, q_ref, kn_ref, vn_ref, *rest, t_new, page, ppb):
    kp_refs, vp_refs = rest[:ppb], rest[ppb:2 * ppb]
    bias_ref, o_ref, qbd_ref, acc_ref, car_ref = rest[2 * ppb:]
    p = pl.program_id(1)
    n_steps = pl.num_programs(1)
    rows = SB_HEADS * t_new
    row_head = lax.broadcasted_iota(jnp.int32, (rows, D_B), 0) // t_new
    col_head = lax.broadcasted_iota(jnp.int32, (rows, D_B), 1) // SB_HEAD_DIM
    blockdiag = row_head == col_head
    cum = _cum_matrix(page, LANES, 0, LANES)

    def block(kb, vb, valid):
        z = lax.dot_general(qbd_ref[...], kb.astype(BF16), (((1,), (1,)), ((), ())),
                            preferred_element_type=F32) + bias_ref[...]
        w, rs = _sb_block(z, valid, cum)
        return jnp.dot(w.astype(BF16), vb.astype(BF16), preferred_element_type=F32), rs

    def accumulate(res):
        for o, rs in res:
            car = car_ref[...]
            e = jnp.exp(car)
            acc_ref[...] += jnp.concatenate([e] * (D_B // LANES), axis=1) * o
            car_ref[...] = car + rs

    @pl.when(p == 0)
    def _():
        scale = SB_HEAD_DIM ** -0.5
        q = q_ref[0] * scale
        qt = jnp.concatenate([q] * SB_HEADS, axis=0)
        qbd_ref[...] = jnp.where(blockdiag, qt, 0.0).astype(BF16)
        acc_ref[...] = jnp.zeros_like(acc_ref)
        car_ref[...] = jnp.zeros_like(car_ref)
        qi = lax.broadcasted_iota(jnp.int32, (rows, page), 0) % t_new
        ki = lax.broadcasted_iota(jnp.int32, (rows, page), 1)
        accumulate([block(kn_ref[0], vn_ref[0], ki < qi)])

    @pl.when(p > 0)
    def _():
        accumulate([block(kp_refs[r][0, 0], vp_refs[r][0, 0], None) for r in range(ppb)])

    @pl.when(p == n_steps - 1)
    def _():
        a = jnp.where(blockdiag, acc_ref[...], 0.0)
        out = a[0:t_new]
        for h in range(1, SB_HEADS):
            out = out + a[h * t_new:(h + 1) * t_new]
        o_ref[0] = out


def attn_sample(q, k_new, v_new, cache_k, cache_v, page_table, bias_rows):
    b, t_new, _ = q.shape
    page = cache_k.shape[2]
    n_pages = page_table.shape[1]
    rows = SB_HEADS * t_new
    ppb = math.gcd(n_pages, SAMPLE_PAGES_PER_STEP)

    def page_map(r):
        return lambda bi, p, pt: (0, pt[bi, n_pages - 1 - ((jnp.maximum(p, 1) - 1) * ppb + r)], 0, 0)

    page_specs = [pl.BlockSpec((1, 1, page, D_B), page_map(r)) for r in range(ppb)]
    grid_spec = pltpu.PrefetchScalarGridSpec(
        num_scalar_prefetch=1,
        grid=(b, n_pages // ppb + 1),
        in_specs=[
            pl.BlockSpec((1, t_new, D_B), lambda bi, p, pt: (bi, 0, 0)),
            pl.BlockSpec((1, page, D_B), lambda bi, p, pt: (bi, 0, 0)),
            pl.BlockSpec((1, page, D_B), lambda bi, p, pt: (bi, 0, 0)),
        ] + page_specs + page_specs + [
            pl.BlockSpec((rows, 1), lambda bi, p, pt: (0, 0)),
        ],
        out_specs=pl.BlockSpec((1, t_new, D_B), lambda bi, p, pt: (bi, 0, 0)),
        scratch_shapes=[pltpu.VMEM((rows, D_B), BF16), pltpu.VMEM((rows, D_B), F32),
                        pltpu.VMEM((rows, LANES), F32)],
    )
    body = functools.partial(_attn_sample_body, t_new=t_new, page=page, ppb=ppb)
    return pl.pallas_call(
        body,
        grid_spec=grid_spec,
        out_shape=jax.ShapeDtypeStruct((b, t_new, D_B), F32),
        compiler_params=_cparams(("arbitrary", "arbitrary")),
        name="attn_sample",
    )(page_table, q, k_new, v_new, *([cache_k] * ppb), *([cache_v] * ppb), bias_rows)


CONV_HALO = 8
CONV_CB = 1024


def _conv_body(cur_ref, halo_ref, buf_ref, w_ref, b_ref, y_ref, nb_ref, win_ref, *, ch):
    c = pl.program_id(2)
    tail = jnp.where(c == 0, buf_ref[0], halo_ref[0])
    win_ref[0:CONV_HALO, :] = tail
    win_ref[CONV_HALO:CONV_HALO + ch, :] = cur_ref[0]
    w = w_ref[...]
    y = b_ref[...]
    for i in range(SSD_CONV):
        off = CONV_HALO - (SSD_CONV - 1) + i
        y = y + win_ref[off:off + ch, :] * w[i:i + 1, :]
    y_ref[0] = y * jax.nn.sigmoid(y)
    nb_ref[0] = win_ref[ch:ch + CONV_HALO, :]


def conv_silu(proj3, buf8, conv_w, conv_b):
    b, t, _ = proj3.shape
    ch = min(128, t)
    nc = t // ch
    col0 = D_INNER // CONV_CB
    ncb = CONV_DIM // CONV_CB
    hb = ch // CONV_HALO
    body = functools.partial(_conv_body, ch=ch)
    return pl.pallas_call(
        body,
        grid=(b, ncb, nc),
        in_specs=[
            pl.BlockSpec((1, ch, CONV_CB), lambda i, j, c: (i, c, col0 + j)),
            pl.BlockSpec((1, CONV_HALO, CONV_CB), lambda i, j, c: (i, jnp.maximum(c * hb - 1, 0), col0 + j)),
            pl.BlockSpec((1, CONV_HALO, CONV_CB), lambda i, j, c: (i, 0, j)),
            pl.BlockSpec((SSD_CONV, CONV_CB), lambda i, j, c: (0, j)),
            pl.BlockSpec((1, CONV_CB), lambda i, j, c: (0, j)),
        ],
        out_specs=[
            pl.BlockSpec((1, ch, CONV_CB), lambda i, j, c: (i, c, j)),
            pl.BlockSpec((1, CONV_HALO, CONV_CB), lambda i, j, c: (i, 0, j)),
        ],
        out_shape=[
            jax.ShapeDtypeStruct((b, t, CONV_DIM), F32),
            jax.ShapeDtypeStruct((b, CONV_HALO, CONV_DIM), F32),
        ],
        scratch_shapes=[pltpu.VMEM((CONV_HALO + ch, CONV_CB), F32)],
        compiler_params=_cparams(("arbitrary", "arbitrary", "arbitrary")),
        name="conv",
    )(proj3, proj3, buf8, conv_w, conv_b)


def _ssd_body(x_ref, b_ref, c_ref, z_ref, dt_ref, dtb_ref, alog_ref, d_ref, ng_ref, h0_ref,
              y_ref, hout_ref, st_ref, *, cl, t_valid):
    c = pl.program_id(1)
    nc = pl.num_programs(1)
    n_pairs = SSD_HEADS // 2
    pairs_per_group = n_pairs // SSD_GROUPS

    @pl.when(c == 0)
    def _():
        for m in range(n_pairs):
            st_ref[m] = h0_ref[0, m].T

    row = lax.broadcasted_iota(jnp.int32, (cl, LANES), 0)
    lane = lax.broadcasted_iota(jnp.int32, (cl, LANES), 1)
    dt = jax.nn.softplus(dt_ref[0] + dtb_ref[...])
    dt = jnp.where((c * cl + row < t_valid) & (lane < SSD_HEADS), dt, 0.0)
    a = -jnp.exp(alog_ref[...])
    da = dt * a
    tri_r = lax.broadcasted_iota(jnp.int32, (cl, cl), 0)
    tri_c = lax.broadcasted_iota(jnp.int32, (cl, cl), 1)
    causal = tri_r >= tri_c
    ltri = jnp.where(causal, 1.0, 0.0).astype(BF16)
    acs = _dot_exact_rhs(ltri, da)
    acs_t = acs.T
    er = lax.broadcasted_iota(jnp.int32, (LANES, D_INNER), 0)
    ec = lax.broadcasted_iota(jnp.int32, (LANES, D_INNER), 1) // SSD_HEAD_DIM
    expand = jnp.where(er == ec, 1.0, 0.0).astype(BF16)
    dt_full = _dot_exact_lhs(dt, expand)
    acs_full = _dot_exact_lhs(acs, expand)
    last_full = acs_full[cl - 1:cl, :]
    xall = x_ref[0]
    xdt = xall * dt_full
    xdt_end = (xdt * jnp.exp(last_full - acs_full)).astype(BF16)
    xdt_b = xdt.astype(BF16)
    e_acs = jnp.exp(acs_full)
    chunk_decay = jnp.exp(last_full)
    lane_lo = lax.broadcasted_iota(jnp.int32, (1, LANES), 1) < SSD_HEAD_DIM
    ys = []
    for g in range(SSD_GROUPS):
        bg = b_ref[0][:, g * SSD_STATE:(g + 1) * SSD_STATE]
        cg = c_ref[0][:, g * SSD_STATE:(g + 1) * SSD_STATE].astype(BF16)
        cb = lax.dot_general(cg, bg.astype(BF16), (((1,), (1,)), ((), ())), preferred_element_type=F32)
        bg_t = bg.T.astype(BF16)
        for mm in range(pairs_per_group):
            m = g * pairs_per_group + mm
            sl = slice(m * LANES, (m + 1) * LANES)
            yd = None
            for hh in range(2):
                h = 2 * m + hh
                seg = acs[:, h:h + 1] - acs_t[h:h + 1, :]
                decay = jnp.exp(jnp.where(causal, seg, NEG_INF))
                sc = (cb * decay).astype(BF16)
                xm = jnp.where(lane_lo if hh == 0 else ~lane_lo, xdt_b[:, sl], jnp.zeros((), BF16))
                d = jnp.dot(sc, xm, preferred_element_type=F32)
                yd = d if yd is None else yd + d
            st = st_ref[m]
            yo = jnp.dot(cg, st.astype(BF16), preferred_element_type=F32) * e_acs[:, sl]
            cs = jnp.dot(bg_t, xdt_end[:, sl], preferred_element_type=F32)
            st_ref[m] = st * chunk_decay[:, sl] + cs
            ys.append(yd + yo)
    y = jnp.concatenate(ys, axis=1) + xall * d_ref[...]
    zz = z_ref[0]
    y = y * (zz * jax.nn.sigmoid(zz))
    gw = D_INNER // SSD_GROUPS
    outs = []
    for g in range(SSD_GROUPS):
        s = y[:, g * gw:(g + 1) * gw]
        r = lax.rsqrt(jnp.mean(s * s, axis=-1, keepdims=True) + EPS)
        outs.append(s * r)
    y_ref[0] = jnp.concatenate(outs, axis=1) * ng_ref[...]

    @pl.when(c == nc - 1)
    def _():
        for m in range(n_pairs):
            hout_ref[0, m] = st_ref[m].T


def ssd_scan(xact, proj3, dt_bias, a_log, d_full, norm_g, h0, t_valid):
    b, t, _ = xact.shape
    cl = SSD_CHUNK
    nc = t // cl
    gn = SSD_GROUPS * SSD_STATE
    body = functools.partial(_ssd_body, cl=cl, t_valid=t_valid)
    n_pairs = SSD_HEADS // 2
    return pl.pallas_call(
        body,
        grid=(b, nc),
        in_specs=[
            pl.BlockSpec((1, cl, D_INNER), lambda i, c: (i, c, 0)),
            pl.BlockSpec((1, cl, gn), lambda i, c: (i, c, D_INNER // gn)),
            pl.BlockSpec((1, cl, gn), lambda i, c: (i, c, D_INNER // gn + 1)),
            pl.BlockSpec((1, cl, D_INNER), lambda i, c: (i, c, 0)),
            pl.BlockSpec((1, cl, LANES), lambda i, c: (i, c, (D_INNER + CONV_DIM) // LANES)),
            pl.BlockSpec((1, LANES), lambda i, c: (0, 0)),
            pl.BlockSpec((1, LANES), lambda i, c: (0, 0)),
            pl.BlockSpec((1, D_INNER), lambda i, c: (0, 0)),
            pl.BlockSpec((1, D_INNER), lambda i, c: (0, 0)),
            pl.BlockSpec((1, n_pairs, LANES, SSD_STATE), lambda i, c: (i, 0, 0, 0)),
        ],
        out_specs=[
            pl.BlockSpec((1, cl, D_INNER), lambda i, c: (i, c, 0)),
            pl.BlockSpec((1, n_pairs, LANES, SSD_STATE), lambda i, c: (i, 0, 0, 0)),
        ],
        out_shape=[
            jax.ShapeDtypeStruct((b, t, D_INNER), F32),
            jax.ShapeDtypeStruct((b, n_pairs, LANES, SSD_STATE), F32),
        ],
        scratch_shapes=[pltpu.VMEM((n_pairs, SSD_STATE, LANES), F32)],
        compiler_params=_cparams(("arbitrary", "arbitrary")),
        name="ssd_scan",
    )(xact, xact, xact, proj3, proj3, dt_bias, a_log, d_full, norm_g, h0)


def _cmp_exchange(vals, i, j):
    hi = jnp.maximum(vals[i], vals[j])
    lo = jnp.minimum(vals[i], vals[j])
    vals[i], vals[j] = hi, lo


def _bitonic_merge_desc(vals):
    n = len(vals)
    j = n // 2
    while j >= 1:
        for i in range(n):
            if i & j == 0:
                _cmp_exchange(vals, i, i | j)
        j //= 2
    return vals


def _bitonic_sort_desc(vals):
    n = len(vals)
    k = 2
    while k <= n:
        j = k // 2
        while j >= 1:
            for i in range(n):
                l = i ^ j
                if l > i:
                    if i & k == 0:
                        _cmp_exchange(vals, i, l)
                    else:
                        _cmp_exchange(vals, l, i)
            j //= 2
        k *= 2
    return vals


def _merge_top(xs, ys):
    n = len(xs)
    return _bitonic_merge_desc([jnp.maximum(xs[i], ys[n - 1 - i]) for i in range(n)])


def _top16_rows(s):
    slabs = [s[SUBLANES * r:SUBLANES * (r + 1), :] for r in range(s.shape[0] // SUBLANES)]
    assert len(slabs) == PEER_TOPK
    top = _bitonic_sort_desc(slabs)
    shift = 1
    while shift < SUBLANES:
        top = _merge_top(top, [pltpu.roll(x, shift, 0) for x in top])
        shift *= 2
    return top


def _top16_of(groups):
    done = [groups[0]] + [_bitonic_sort_desc(list(gp)) for gp in groups[1:]]
    while len(done) > 1:
        done = [_merge_top(done[i], done[i + 1]) if i + 1 < len(done) else done[i]
                for i in range(0, len(done), 2)]
    return done[0]


def _peer_body(x_ref, g_ref, sh_ref, sc_ref, gt_ref, wq_ref, k1_ref, k2_ref, u_ref, v_ref, og_ref,
               o_ref, ht_ref, r2_ref, f2_ref, rho_ref, f1_ref, p_ref, acc_ref, *, tt, eb, final_norm):
    j = pl.program_id(1)
    nj = pl.num_programs(1)
    ncol = tt // LANES
    k = PEER_TOPK

    @pl.when(j == 0)
    def _():
        h = _modulate(x_ref[...], g_ref[...], sh_ref[0], sc_ref[0])
        ht_ref[...] = h.T.astype(BF16)
        acc_ref[...] = jnp.zeros_like(acc_ref)

        def per_head(hd, _):
            r0 = pl.multiple_of(hd * D_KEY, D_KEY)
            qh = jnp.dot(wq_ref[pl.ds(r0, D_KEY), :], ht_ref[...], preferred_element_type=F32)
            s1 = jnp.dot(k1_ref[hd], qh[:D_HALF].astype(BF16), preferred_element_type=F32)
            s2 = jnp.dot(k2_ref[hd], qh[D_HALF:].astype(BF16), preferred_element_type=F32)
            v1 = _top16_rows(s1)
            v2 = _top16_rows(s2)
            width = [k // (a_ + 1) for a_ in range(k)]
            cand = [[v1[a_] + v2[b_] for b_ in range(width[a_])] for a_ in range(k)]
            rest = [c for row in cand[1:] for c in row]
            rest += [jnp.full_like(v1[0], NEG_INF)] * (-len(rest) % k)
            best = _top16_of([cand[0]] + [rest[i:i + k] for i in range(0, len(rest), k)])
            tau = best[k - 1]
            zsum = None
            for b_ in best:
                e = jnp.exp(b_ - best[0])
                zsum = e if zsum is None else zsum + e
            quota = []
            for a_ in range(k):
                cnt = jnp.zeros_like(tau)
                for b_ in range(width[a_]):
                    cnt = cnt + jnp.where(cand[a_][b_] >= tau, 1.0, 0.0)
                quota.append(cnt)
            inv_z = 1.0 / zsum

            def key_codes(r):
                sl = slice(SUBLANES * r, SUBLANES * (r + 1))
                s1r, s2r = s1[sl], s2[sl]
                rank2 = jnp.zeros_like(s2r)
                for b_ in range(k):
                    rank2 = rank2 + jnp.where(v2[b_] > s2r, 1.0, 0.0)
                rho = jnp.zeros_like(s1r)
                for a_ in reversed(range(k)):
                    rho = jnp.where(s1r == v1[a_], quota[a_], rho)
                return rank2, jnp.exp(s2r - v2[0]), rho, jnp.exp(s1r - v1[0]) * inv_z

            for r in range(0, N_KEYS // SUBLANES, 2):
                lo, hi = key_codes(r), key_codes(r + 1)
                rank2, f2, rho, f1 = [jnp.concatenate([x, y], axis=0) for x, y in zip(lo, hi)]
                sl = slice(SUBLANES * r, SUBLANES * (r + 2))
                for cc in range(ncol):
                    cs = slice(cc * LANES, (cc + 1) * LANES)
                    r2_ref[hd, cc, sl, :] = rank2[:, cs].astype(BF16)
                    f2_ref[hd, cc, sl, :] = f2[:, cs].astype(BF16)
                    rho_ref[hd, cc, sl, :] = rho[:, cs]
                    f1_ref[hd, cc, sl, :] = f1[:, cs]
            return 0

        lax.fori_loop(0, PEER_HEADS, per_head, 0)

    a = jnp.dot(u_ref[...], ht_ref[...], preferred_element_type=F32)
    act = 0.5 * a * (1.0 + lax.erf(a * (2.0 ** -0.5)))
    n_i1 = eb // N_KEYS
    for ii in range(n_i1):
        i1 = j * n_i1 + ii
        for cc in range(ncol):
            w = jnp.zeros((N_KEYS, LANES), BF16)
            for hd in range(PEER_HEADS):
                rho = jnp.broadcast_to(rho_ref[hd, cc, pl.ds(i1, 1), :], (N_KEYS, LANES)).astype(BF16)
                f1r = jnp.broadcast_to(f1_ref[hd, cc, pl.ds(i1, 1), :], (N_KEYS, LANES)).astype(BF16)
                sel = r2_ref[hd, cc] < rho
                w = w + jnp.where(sel, f1r * f2_ref[hd, cc], jnp.zeros((), BF16))
            tile = act[ii * N_KEYS:(ii + 1) * N_KEYS, cc * LANES:(cc + 1) * LANES].astype(BF16)
            p_ref[ii * N_KEYS:(ii + 1) * N_KEYS, cc * LANES:(cc + 1) * LANES] = w * tile
    acc_ref[...] += lax.dot_general(p_ref[...], v_ref[...], (((0,), (0,)), ((), ())),
                                    preferred_element_type=F32)

    @pl.when(j == nj - 1)
    def _():
        y = x_ref[...] + gt_ref[0] * acc_ref[...]
        if final_norm:
            r = lax.rsqrt(jnp.mean(y * y, axis=-1, keepdims=True) + EPS)
            y = (y * r) * og_ref[...]
        o_ref[...] = y


def peer_layer(x, g, sh, sc, gt, wq_t, k1, k2, u, v, out_g, tt, eb=512, final_norm=False):
    n = x.shape[0]
    nt = n // tt
    ncol = tt // LANES
    nj = N_EXPERTS // eb
    body = functools.partial(_peer_body, tt=tt, eb=eb, final_norm=final_norm)

    def grp(a):
        gg, r, c = a.shape
        tpg = nt // gg
        return pl.BlockSpec((1, r, c), lambda i, j: (i // tpg, 0, 0))

    return pl.pallas_call(
        body,
        grid=(nt, nj),
        in_specs=[
            pl.BlockSpec((tt, D_MODEL), lambda i, j: (i, 0)),
            pl.BlockSpec((1, D_MODEL), lambda i, j: (0, 0)),
            grp(sh), grp(sc), grp(gt),
            pl.BlockSpec(wq_t.shape, lambda i, j: (0, 0)),
            pl.BlockSpec(k1.shape, lambda i, j: (0, 0, 0)),
            pl.BlockSpec(k2.shape, lambda i, j: (0, 0, 0)),
            pl.BlockSpec((eb, D_MODEL), lambda i, j: (j, 0)),
            pl.BlockSpec((eb, D_MODEL), lambda i, j: (j, 0)),
            pl.BlockSpec((1, D_MODEL), lambda i, j: (0, 0)),
        ],
        out_specs=pl.BlockSpec((tt, D_MODEL), lambda i, j: (i, 0)),
        out_shape=jax.ShapeDtypeStruct((n, D_MODEL), F32),
        scratch_shapes=[
            pltpu.VMEM((D_MODEL, tt), BF16),
            pltpu.VMEM((PEER_HEADS, ncol, N_KEYS, LANES), BF16),
            pltpu.VMEM((PEER_HEADS, ncol, N_KEYS, LANES), BF16),
            pltpu.VMEM((PEER_HEADS, ncol, N_KEYS, LANES), F32),
            pltpu.VMEM((PEER_HEADS, ncol, N_KEYS, LANES), F32),
            pltpu.VMEM((eb, tt), BF16),
            pltpu.VMEM((tt, D_MODEL), F32),
        ],
        compiler_params=_cparams(("arbitrary", "arbitrary")),
        name="peer",
    )(x, g, sh, sc, gt, wq_t, k1, k2, u, v, out_g)


def _row_groups(a, bsz, t, tm):
    if t % tm == 0:
        return a[:, None, :]
    assert tm % t == 0
    per_row = jnp.repeat(a, t, axis=0)
    return per_row.reshape(bsz * t // tm, tm, a.shape[1])


def _trunk(x, c_mod, start_pos, past, pool_buf, conv_buf, ssm_state, wts, tm, tt):
    bsz, t, _ = x.shape
    n = bsz * t
    x2 = x.reshape(n, D_MODEL)
    rg = lambda a: _row_groups(a, bsz, t, tm)
    rgp = lambda a: _row_groups(a, bsz, t, tt)

    sh1, sc1, g1, sh2, sc2, g2 = c_mod[0]
    proj = matmul([x2], [wts["ab_w_in"]], tm, pro=(wts["norm_mix_g"][0], rg(sh1), rg(sc1)), name="ab_in")
    proj3 = proj.reshape(bsz, t, -1)
    k_new = proj3[..., D_A + D_B:D_A + 2 * D_B]
    v_new = proj3[..., D_A + 2 * D_B:]
    buf16 = jnp.pad(pool_buf, ((0, 0), (POOL_HALO - POOL_BUF, 0), (0, 0)))
    ya, nb16 = pool_mixer(proj3, buf16, wts["pool_wbd"], wts["pool_scale"], start_pos)
    new_pool = nb16[:, POOL_HALO - POOL_BUF:]
    if past is None:
        yb = attn_prompt(proj3, wts["sb_bias"])
    else:
        cache_k, cache_v, page_table = past
        page = cache_k.shape[2]
        q = proj3[..., D_A:D_A + D_B]
        padr = ((0, 0), (0, page - t), (0, 0))
        bias_rows = jnp.repeat(wts["sb_bias"], t)[:, None]
        yb = attn_sample(q, jnp.pad(k_new, padr), jnp.pad(v_new, padr), cache_k, cache_v, page_table, bias_rows)
    x2 = matmul([ya.reshape(n, D_A), yb.reshape(n, D_B)], [wts["ab_w_out_a"], wts["ab_w_out_b"]], tm,
                res=(x2, rg(g1)), name="ab_out")
    x2 = peer_layer(x2, wts["norm_ffn_g"][0], rgp(sh2), rgp(sc2), rgp(g2), wts["peer_wq_t"][0],
                    wts["peer_k1"][0], wts["peer_k2"][0], wts["peer_u"][0], wts["peer_v"][0],
                    wts["norm_out_g"], tt)

    sh1, sc1, g1, sh2, sc2, g2 = c_mod[1]
    proj = matmul([x2], [wts["ssd_w_in"]], tm, pro=(wts["norm_mix_g"][1], rg(sh1), rg(sc1)), name="ssd_in")
    proj3 = proj.reshape(bsz, t, -1)
    buf8 = jnp.pad(conv_buf, ((0, 0), (CONV_HALO - (SSD_CONV - 1), 0), (0, 0)))
    xact, nc8 = conv_silu(proj3, buf8, wts["ssd_conv_w"], wts["ssd_conv_b"])
    new_conv = nc8[:, CONV_HALO - (SSD_CONV - 1):]
    tp = -(-t // SSD_CHUNK) * SSD_CHUNK
    if tp != t:
        padt = ((0, 0), (0, tp - t), (0, 0))
        xact = jnp.pad(xact, padt)
        proj3 = jnp.pad(proj3, padt)
    h0 = ssm_state.reshape(bsz, SSD_HEADS // 2, LANES, SSD_STATE)
    ynorm, hlast = ssd_scan(xact, proj3, wts["ssd_dt_bias"], wts["ssd_a_log"], wts["ssd_d_full"],
                            wts["ssd_norm_g"], h0, t)
    new_ssm = hlast.reshape(bsz, SSD_HEADS, SSD_HEAD_DIM, SSD_STATE)
    ynorm = ynorm[:, :t].reshape(n, D_INNER)
    x2 = matmul([ynorm], [wts["ssd_w_out"]], tm, res=(x2, rg(g1)), name="ssd_out")
    x2 = peer_layer(x2, wts["norm_ffn_g"][1], rgp(sh2), rgp(sc2), rgp(g2), wts["peer_wq_t"][1],
                    wts["peer_k1"][1], wts["peer_k2"][1], wts["peer_u"][1], wts["peer_v"][1],
                    wts["norm_out_g"], tt, final_norm=True)
    return (x2.reshape(bsz, t, D_MODEL), k_new.reshape(bsz, t, SB_HEADS, SB_HEAD_DIM)[None],
            v_new.reshape(bsz, t, SB_HEADS, SB_HEAD_DIM)[None], new_pool[None], new_conv[None], new_ssm[None])


def kernel(x_prompt, x_sample, cache_sb_k, cache_sb_v, state_pool, state_conv, state_ssm, page_table,
           c_prompt, c_sample, ada_w, ada_b, norm_mix_g, norm_ffn_g, norm_out_g,
           ab_w_in, ab_w_out, pool_w, pool_scale, sb_bias,
           ssd_w_in, ssd_conv_w, ssd_conv_b, ssd_dt_bias, ssd_a_log, ssd_d, ssd_norm_g, ssd_w_out,
           peer_wq, peer_k1, peer_k2, peer_u, peer_v):
    bsz, seq, _ = x_prompt.shape
    dbsz, dseq, _ = x_sample.shape
    depth = ada_w.shape[0]
    assert depth == 2 and ab_w_in.shape[0] == 1 and ssd_w_in.shape[0] == 1

    ssd_pad = (-D_SSD_IN) % LANES
    pad_lanes = lambda a: jnp.pad(a, ((0, 0), (0, LANES - a.shape[1])))
    wts = {
        "norm_mix_g": [norm_mix_g[l][None] for l in range(depth)],
        "norm_ffn_g": [norm_ffn_g[l][None] for l in range(depth)],
        "norm_out_g": norm_out_g[None],
        "ab_w_in": ab_w_in[0].astype(BF16),
        "ab_w_out_a": ab_w_out[0, :D_A].astype(BF16),
        "ab_w_out_b": ab_w_out[0, D_A:].astype(BF16),
        "pool_wbd": jax.scipy.linalg.block_diag(*[pool_w[0, g] for g in range(len(POOL_WINDOWS))]).astype(BF16),
        "pool_scale": pool_scale[0][None],
        "sb_bias": sb_bias[0],
        "ssd_w_in": jnp.pad(ssd_w_in[0], ((0, 0), (0, ssd_pad))).astype(BF16),
        "ssd_conv_w": ssd_conv_w[0],
        "ssd_conv_b": ssd_conv_b[0][None],
        "ssd_dt_bias": pad_lanes(ssd_dt_bias[0][None]),
        "ssd_a_log": pad_lanes(ssd_a_log[0][None]),
        "ssd_d_full": jnp.repeat(ssd_d[0], SSD_HEAD_DIM)[None],
        "ssd_norm_g": ssd_norm_g[0][None],
        "ssd_w_out": ssd_w_out[0].astype(BF16),
        "peer_wq_t": [peer_wq[l].T.astype(BF16) for l in range(depth)],
        "peer_k1": [peer_k1[l].astype(BF16) for l in range(depth)],
        "peer_k2": [peer_k2[l].astype(BF16) for l in range(depth)],
        "peer_u": [peer_u[l].astype(BF16) for l in range(depth)],
        "peer_v": [peer_v[l].astype(BF16) for l in range(depth)],
    }

    c_all = jnp.concatenate([c_prompt, c_sample], axis=0)
    mods_p, mods_s = [], []
    for l in range(depth):
        mod = matmul([c_all], [ada_w[l].astype(BF16)], c_all.shape[0], bias=ada_b[l][None], name="ada")
        parts = jnp.split(mod, 6, axis=-1)
        mods_p.append([p[:bsz] for p in parts])
        mods_s.append([p[bsz:] for p in parts])

    zeros = lambda shape: jnp.zeros(shape, F32)
    out_p = _trunk(x_prompt, mods_p, 0, None, zeros((bsz, POOL_BUF, D_A)), zeros((bsz, SSD_CONV - 1, CONV_DIM)),
                   zeros((bsz, SSD_HEADS, SSD_HEAD_DIM, SSD_STATE)), wts, tm=256, tt=512)
    n_pool, page = cache_sb_k.shape[1], cache_sb_k.shape[2]
    past = (cache_sb_k.reshape(1, n_pool, page, D_B), cache_sb_v.reshape(1, n_pool, page, D_B), page_table)
    past_len = page_table.shape[1] * page
    out_s = _trunk(x_sample, mods_s, past_len, past, state_pool[0], state_conv[0], state_ssm[0], wts,
                   tm=dbsz * dseq, tt=dbsz * dseq)
    y_p, k_p, v_p, pool_p, conv_p, ssm_p = out_p
    y_s, k_s, v_s, pool_s, conv_s, ssm_s = out_s
    return (y_p, y_s, k_p, v_p, k_s, v_s, pool_p, pool_s, conv_p, conv_s, ssm_p, ssm_s)
```

```python
import functools
import math

import jax
import jax.numpy as jnp
from jax import lax
from jax.experimental import pallas as pl
from jax.experimental.pallas import tpu as pltpu

F32 = jnp.float32
BF16 = jnp.bfloat16

D_MODEL = 1024
POOL_WINDOWS = (2, 4, 8, 16)
D_A = 256
POOL_GC = 64
POOL_BUF = 15
SB_HEAD_DIM = 64
D_B = 768
SB_HEADS = 12
D_INNER = 2048
SSD_HEAD_DIM = 64
SSD_HEADS = 32
SSD_GROUPS = 4
SSD_STATE = 128
SSD_CONV = 4
SSD_CHUNK = 128
CONV_DIM = 3072
D_SSD_IN = D_INNER + CONV_DIM + SSD_HEADS
PEER_HEADS = 8
N_KEYS = 128
N_EXPERTS = N_KEYS * N_KEYS
PEER_TOPK = 16
D_KEY = 256
D_HALF = 128
EPS = 1e-6

LANES = 128
SUBLANES = 8
VMEM_LIMIT = 56 * 1024 * 1024

NEG_INF = float("-inf")


def _cparams(sem):
    return pltpu.CompilerParams(dimension_semantics=sem, vmem_limit_bytes=VMEM_LIMIT)


def _modulate(x, g, shift, scale):
    r = lax.rsqrt(jnp.mean(x * x, axis=-1, keepdims=True) + EPS)
    return (x * r) * g * (1.0 + scale) + shift


def _split3(x):
    hi = x.astype(BF16)
    r1 = x - hi.astype(F32)
    mid = r1.astype(BF16)
    lo = (r1 - mid.astype(F32)).astype(BF16)
    return hi, mid, lo


def _dot_exact_rhs(a_bf16, x):
    hi, mid, lo = _split3(x)
    return (jnp.dot(a_bf16, hi, preferred_element_type=F32)
            + jnp.dot(a_bf16, mid, preferred_element_type=F32)
            + jnp.dot(a_bf16, lo, preferred_element_type=F32))


def _dot_exact_lhs(x, a_bf16):
    hi, mid, lo = _split3(x)
    return (jnp.dot(hi, a_bf16, preferred_element_type=F32)
            + jnp.dot(mid, a_bf16, preferred_element_type=F32)
            + jnp.dot(lo, a_bf16, preferred_element_type=F32))


def _mm_body(*refs, n_x, has_pro, has_bias, has_res):
    it = iter(refs)
    xs = [next(it) for _ in range(n_x)]
    ws = [next(it) for _ in range(n_x)]
    if has_pro:
        g_ref, sh_ref, sc_ref = next(it), next(it), next(it)
    if has_bias:
        b_ref = next(it)
    if has_res:
        r_ref, gt_ref = next(it), next(it)
    o_ref = next(it)
    acc = None
    for x_ref, w_ref in zip(xs, ws):
        x = x_ref[...]
        if has_pro:
            x = _modulate(x, g_ref[...], sh_ref[0], sc_ref[0])
        d = jnp.dot(x.astype(BF16), w_ref[...], preferred_element_type=F32)
        acc = d if acc is None else acc + d
    if has_bias:
        acc = acc + b_ref[...]
    if has_res:
        acc = r_ref[...] + gt_ref[0] * acc
    o_ref[...] = acc


def _group_spec(a, n_tiles):
    g, r, c = a.shape
    tiles_per_group = n_tiles // g
    return pl.BlockSpec((1, r, c), lambda i: (i // tiles_per_group, 0, 0))


def matmul(xs, ws, tm, pro=None, bias=None, res=None, name="mm"):
    m = xs[0].shape[0]
    n = ws[0].shape[1]
    nt = m // tm
    assert nt * tm == m
    in_specs = [pl.BlockSpec((tm, x.shape[1]), lambda i: (i, 0)) for x in xs]
    in_specs += [pl.BlockSpec(w.shape, lambda i: (0, 0)) for w in ws]
    args = list(xs) + list(ws)
    if pro is not None:
        g, sh, sc = pro
        in_specs += [pl.BlockSpec(g.shape, lambda i: (0, 0)), _group_spec(sh, nt), _group_spec(sc, nt)]
        args += [g, sh, sc]
    if bias is not None:
        in_specs.append(pl.BlockSpec(bias.shape, lambda i: (0, 0)))
        args.append(bias)
    if res is not None:
        r, gt = res
        in_specs += [pl.BlockSpec((tm, n), lambda i: (i, 0)), _group_spec(gt, nt)]
        args += [r, gt]
    body = functools.partial(_mm_body, n_x=len(xs), has_pro=pro is not None,
                             has_bias=bias is not None, has_res=res is not None)
    return pl.pallas_call(
        body,
        grid=(nt,),
        in_specs=in_specs,
        out_specs=pl.BlockSpec((tm, n), lambda i: (i, 0)),
        out_shape=jax.ShapeDtypeStruct((m, n), F32),
        compiler_params=_cparams(("arbitrary",)),
        name=name,
    )(*args)


POOL_HALO = 16


def _pool_body(cur_ref, prev_ref, buf_ref, wbd_ref, ps_ref, y_ref, nb_ref, win_ref, *, ch, start_pos):
    c = pl.program_id(1)
    cur = cur_ref[0]
    if ch >= POOL_HALO:
        tail = jnp.where(c == 0, buf_ref[0], prev_ref[0][ch - POOL_HALO:ch])
    else:
        tail = buf_ref[0]
    win_ref[0:POOL_HALO, :] = tail
    win_ref[POOL_HALO:POOL_HALO + ch, :] = cur

    def shifted(k):
        return win_ref[POOL_HALO - k:POOL_HALO - k + ch, :]

    pos = start_pos + c * ch + lax.broadcasted_iota(jnp.int32, (ch, 1), 0)
    grp = lax.broadcasted_iota(jnp.int32, (1, D_A), 1) // POOL_GC
    s = cur
    mean = jnp.zeros_like(cur)
    k = 1
    for g, w in enumerate(POOL_WINDOWS):
        while k < w:
            s = s + shifted(k)
            k += 1
        cnt = jnp.minimum(pos + 1, w).astype(F32)
        mean = jnp.where(grp == g, s / cnt, mean)
    diff = mean - cur
    y = jnp.dot(diff.astype(BF16), wbd_ref[...], preferred_element_type=F32) * ps_ref[...]
    y_ref[0] = y
    nb_ref[0] = win_ref[ch:ch + POOL_HALO, :]


def pool_mixer(proj3, buf16, wbd, pscale, start_pos):
    b, t, _ = proj3.shape
    ch = min(128, t)
    nc = t // ch
    body = functools.partial(_pool_body, ch=ch, start_pos=start_pos)
    return pl.pallas_call(
        body,
        grid=(b, nc),
        in_specs=[
            pl.BlockSpec((1, ch, D_A), lambda i, c: (i, c, 0)),
            pl.BlockSpec((1, ch, D_A), lambda i, c: (i, jnp.maximum(c - 1, 0), 0)),
            pl.BlockSpec((1, POOL_HALO, D_A), lambda i, c: (i, 0, 0)),
            pl.BlockSpec((D_A, D_A), lambda i, c: (0, 0)),
            pl.BlockSpec((1, D_A), lambda i, c: (0, 0)),
        ],
        out_specs=[
            pl.BlockSpec((1, ch, D_A), lambda i, c: (i, c, 0)),
            pl.BlockSpec((1, POOL_HALO, D_A), lambda i, c: (i, 0, 0)),
        ],
        out_shape=[
            jax.ShapeDtypeStruct((b, t, D_A), F32),
            jax.ShapeDtypeStruct((b, POOL_HALO, D_A), F32),
        ],
        scratch_shapes=[pltpu.VMEM((POOL_HALO + ch, D_A), F32)],
        compiler_params=_cparams(("arbitrary", "arbitrary")),
        name="pool",
    )(proj3, proj3, buf16, wbd, pscale)


SAMPLE_PAGES_PER_STEP = 4


def _cum_matrix(tk, n_ones, ones_lo, ones_hi):
    r = lax.broadcasted_iota(jnp.int32, (tk, tk + n_ones), 0)
    c = lax.broadcasted_iota(jnp.int32, (tk, tk + n_ones), 1)
    ones = (c >= tk + ones_lo) & (c < tk + ones_hi)
    return jnp.where(ones | ((c < tk) & (r > c)), 1.0, 0.0).astype(BF16)


def _sb_block(z, valid, cum):
    tk = z.shape[1]
    l1p = jnp.log1p(jnp.exp(-jnp.abs(z)))
    log_beta = jnp.minimum(z, 0.0) - l1p
    log_keep = jnp.minimum(-z, 0.0) - l1p
    if valid is not None:
        log_keep = jnp.where(valid, log_keep, 0.0)
    hi = log_keep.astype(BF16)
    lo = (log_keep - hi.astype(F32)).astype(BF16)
    cs = (jnp.dot(hi, cum, preferred_element_type=F32)
          + jnp.dot(lo, cum, preferred_element_type=F32))
    w = jnp.exp(log_beta + cs[:, :tk])
    if valid is not None:
        w = jnp.where(valid, w, 0.0)
    return w, cs[:, tk:]


def _attn_prompt_body(bias_ref, q_ref, k_ref, v_ref, o_ref, acc_ref, car_ref, *, tq, tk):
    g = pl.program_id(1)
    i = pl.program_id(2)
    scale = SB_HEAD_DIM ** -0.5
    q = q_ref[0] * scale
    lane = lax.broadcasted_iota(jnp.int32, (1, LANES), 1)
    head_mask = [lane < SB_HEAD_DIM, lane >= SB_HEAD_DIM]
    qm = [jnp.where(m, q, 0.0).astype(BF16) for m in head_mask]
    bias = [bias_ref[2 * g], bias_ref[2 * g + 1]]
    cum = [_cum_matrix(tk, LANES, 0, SB_HEAD_DIM), _cum_matrix(tk, LANES, SB_HEAD_DIM, LANES)]
    acc_ref[...] = jnp.zeros_like(acc_ref)
    car_ref[...] = jnp.zeros_like(car_ref)
    qpos = i * tq + lax.broadcasted_iota(jnp.int32, (tq, tk), 0)
    kiota = lax.broadcasted_iota(jnp.int32, (tq, tk), 1)
    blocks_per_step = tq // tk

    def one_block(k0, masked):
        kb = k_ref[0, pl.ds(k0, tk), :].astype(BF16)
        vb = v_ref[0, pl.ds(k0, tk), :]
        valid = ((k0 + kiota) < qpos) if masked else None
        ws, rs = [], None
        for h in range(2):
            z = lax.dot_general(qm[h], kb, (((1,), (1,)), ((), ())), preferred_element_type=F32) + bias[h]
            w, r = _sb_block(z, valid, cum[h])
            ws.append(w.astype(BF16))
            rs = r if rs is None else rs + r
        vm = jnp.concatenate([jnp.where(head_mask[h], vb, 0.0).astype(BF16) for h in range(2)], axis=0)
        o = jnp.dot(jnp.concatenate(ws, axis=1), vm, preferred_element_type=F32)
        return o, rs

    def step(first_block, masked):
        res = [one_block(pl.multiple_of(first_block * tk + r * tk, tk), masked)
               for r in range(blocks_per_step)]
        for o, rs in reversed(res):
            car = car_ref[...]
            acc_ref[...] += jnp.exp(car) * o
            car_ref[...] = car + rs

    step(i * blocks_per_step, True)

    def body(jj, _):
        step((i - 1 - jj) * blocks_per_step, False)
        return 0

    lax.fori_loop(0, i, body, 0)
    o_ref[0] = acc_ref[...]


def attn_prompt(proj3, sb_bias, tq=256, tk=128):
    b, t, _ = proj3.shape
    tq = min(tq, t)
    tk = min(tk, tq)
    n_pairs = SB_HEADS // 2
    qo = D_A // LANES
    body = functools.partial(_attn_prompt_body, tq=tq, tk=tk)
    return pl.pallas_call(
        body,
        grid=(b, n_pairs, t // tq),
        in_specs=[
            pl.BlockSpec(memory_space=pltpu.SMEM),
            pl.BlockSpec((1, tq, LANES), lambda bi, g, i: (bi, i, qo + g)),
            pl.BlockSpec((1, t, LANES), lambda bi, g, i: (bi, 0, qo + n_pairs + g)),
            pl.BlockSpec((1, t, LANES), lambda bi, g, i: (bi, 0, qo + 2 * n_pairs + g)),
        ],
        out_specs=pl.BlockSpec((1, tq, LANES), lambda bi, g, i: (bi, i, g)),
        scratch_shapes=[pltpu.VMEM((tq, LANES), F32), pltpu.VMEM((tq, LANES), F32)],
        out_shape=jax.ShapeDtypeStruct((b, t, D_B), F32),
        compiler_params=_cparams(("arbitrary", "arbitrary", "arbitrary")),
        name="attn_prompt",
    )(sb_bias, proj3, proj3, proj3)


def _attn_sample_body(pt_ref, q_ref, kn_ref, vn_ref, *rest, t_new, page, ppb):
    kp_refs, vp_refs = rest[:ppb], rest[ppb:2 * ppb]
    bias_ref, o_ref, qbd_ref, acc_ref, car_ref = rest[2 * ppb:]
    p = pl.program_id(1)
    n_steps = pl.num_programs(1)
    rows = SB_HEADS * t_new
    row_head = lax.broadcasted_iota(jnp.int32, (rows, D_B), 0) // t_new
    col_head = lax.broadcasted_iota(jnp.int32, (rows, D_B), 1) // SB_HEAD_DIM
    blockdiag = row_head == col_head
    cum = _cum_matrix(page, LANES, 0, LANES)

    def block(kb, vb, valid):
        z = lax.dot_general(qbd_ref[...], kb.astype(BF16), (((1,), (1,)), ((), ())),
                            preferred_element_type=F32) + bias_ref[...]
        w, rs = _sb_block(z, valid, cum)
        return jnp.dot(w.astype(BF16), vb.astype(BF16), preferred_element_type=F32), rs

    def accumulate(res):
        for o, rs in res:
            car = car_ref[...]
            e = jnp.exp(car)
            acc_ref[...] += jnp.concatenate([e] * (D_B // LANES), axis=1) * o
            car_ref[...] = car + rs

    @pl.when(p == 0)
    def _():
        scale = SB_HEAD_DIM ** -0.5
        q = q_ref[0] * scale
        qt = jnp.concatenate([q] * SB_HEADS, axis=0)
        qbd_ref[...] = jnp.where(blockdiag, qt, 0.0).astype(BF16)
        acc_ref[...] = jnp.zeros_like(acc_ref)
        car_ref[...] = jnp.zeros_like(car_ref)
        qi = lax.broadcasted_iota(jnp.int32, (rows, page), 0) % t_new
        ki = lax.broadcasted_iota(jnp.int32, (rows, page), 1)
        accumulate([block(kn_ref[0], vn_ref[0], ki < qi)])

    @pl.when(p > 0)
    def _():
        accumulate([block(kp_refs[r][0, 0], vp_refs[r][0, 0], None) for r in range(ppb)])

    @pl.when(p == n_steps - 1)
    def _():
        a = jnp.where(blockdiag, acc_ref[...], 0.0)
        out = a[0:t_new]
        for h in range(1, SB_HEADS):
            out = out + a[h * t_new:(h + 1) * t_new]
        o_ref[0] = out


def attn_sample(q, k_new, v_new, cache_k, cache_v, page_table, bias_rows):
    b, t_new, _ = q.shape
    page = cache_k.shape[2]
    n_pages = page_table.shape[1]
    rows = SB_HEADS * t_new
    ppb = math.gcd(n_pages, SAMPLE_PAGES_PER_STEP)

    def page_map(r):
        return lambda bi, p, pt: (0, pt[bi, n_pages - 1 - ((jnp.maximum(p, 1) - 1) * ppb + r)], 0, 0)

    page_specs = [pl.BlockSpec((1, 1, page, D_B), page_map(r)) for r in range(ppb)]
    grid_spec = pltpu.PrefetchScalarGridSpec(
        num_scalar_prefetch=1,
        grid=(b, n_pages // ppb + 1),
        in_specs=[
            pl.BlockSpec((1, t_new, D_B), lambda bi, p, pt: (bi, 0, 0)),
            pl.BlockSpec((1, page, D_B), lambda bi, p, pt: (bi, 0, 0)),
            pl.BlockSpec((1, page, D_B), lambda bi, p, pt: (bi, 0, 0)),
        ] + page_specs + page_specs + [
            pl.BlockSpec((rows, 1), lambda bi, p, pt: (0, 0)),
        ],
        out_specs=pl.BlockSpec((1, t_new, D_B), lambda bi, p, pt: (bi, 0, 0)),
        scratch_shapes=[pltpu.VMEM((rows, D_B), BF16), pltpu.VMEM((rows, D_B), F32),
                        pltpu.VMEM((rows, LANES), F32)],
    )
    body = functools.partial(_attn_sample_body, t_new=t_new, page=page, ppb=ppb)
    return pl.pallas_call(
        body,
        grid_spec=grid_spec,
        out_shape=jax.ShapeDtypeStruct((b, t_new, D_B), F32),
        compiler_params=_cparams(("arbitrary", "arbitrary")),
        name="attn_sample",
    )(page_table, q, k_new, v_new, *([cache_k] * ppb), *([cache_v] * ppb), bias_rows)


CONV_HALO = 8
CONV_CB = 1024


def _conv_body(cur_ref, halo_ref, buf_ref, w_ref, b_ref, y_ref, nb_ref, win_ref, *, ch):
    c = pl.program_id(2)
    tail = jnp.where(c == 0, buf_ref[0], halo_ref[0])
    win_ref[0:CONV_HALO, :] = tail
    win_ref[CONV_HALO:CONV_HALO + ch, :] = cur_ref[0]
    w = w_ref[...]
    y = b_ref[...]
    for i in range(SSD_CONV):
        off = CONV_HALO - (SSD_CONV - 1) + i
        y = y + win_ref[off:off + ch, :] * w[i:i + 1, :]
    y_ref[0] = y * jax.nn.sigmoid(y)
    nb_ref[0] = win_ref[ch:ch + CONV_HALO, :]


def conv_silu(proj3, buf8, conv_w, conv_b):
    b, t, _ = proj3.shape
    ch = min(128, t)
    nc = t // ch
    col0 = D_INNER // CONV_CB
    ncb = CONV_DIM // CONV_CB
    hb = ch // CONV_HALO
    body = functools.partial(_conv_body, ch=ch)
    return pl.pallas_call(
        body,
        grid=(b, ncb, nc),
        in_specs=[
            pl.BlockSpec((1, ch, CONV_CB), lambda i, j, c: (i, c, col0 + j)),
            pl.BlockSpec((1, CONV_HALO, CONV_CB), lambda i, j, c: (i, jnp.maximum(c * hb - 1, 0), col0 + j)),
            pl.BlockSpec((1, CONV_HALO, CONV_CB), lambda i, j, c: (i, 0, j)),
            pl.BlockSpec((SSD_CONV, CONV_CB), lambda i, j, c: (0, j)),
            pl.BlockSpec((1, CONV_CB), lambda i, j, c: (0, j)),
        ],
        out_specs=[
            pl.BlockSpec((1, ch, CONV_CB), lambda i, j, c: (i, c, j)),
            pl.BlockSpec((1, CONV_HALO, CONV_CB), lambda i, j, c: (i, 0, j)),
        ],
        out_shape=[
            jax.ShapeDtypeStruct((b, t, CONV_DIM), F32),
            jax.ShapeDtypeStruct((b, CONV_HALO, CONV_DIM), F32),
        ],
        scratch_shapes=[pltpu.VMEM((CONV_HALO + ch, CONV_CB), F32)],
        compiler_params=_cparams(("arbitrary", "arbitrary", "arbitrary")),
        name="conv",
    )(proj3, proj3, buf8, conv_w, conv_b)


def _ssd_body(x_ref, b_ref, c_ref, z_ref, dt_ref, dtb_ref, alog_ref, d_ref, ng_ref, h0_ref,
              y_ref, hout_ref, st_ref, *, cl, t_valid):
    c = pl.program_id(1)
    nc = pl.num_programs(1)
    n_pairs = SSD_HEADS // 2
    pairs_per_group = n_pairs // SSD_GROUPS

    @pl.when(c == 0)
    def _():
        for m in range(n_pairs):
            st_ref[m] = h0_ref[0, m].T

    row = lax.broadcasted_iota(jnp.int32, (cl, LANES), 0)
    lane = lax.broadcasted_iota(jnp.int32, (cl, LANES), 1)
    dt = jax.nn.softplus(dt_ref[0] + dtb_ref[...])
    dt = jnp.where((c * cl + row < t_valid) & (lane < SSD_HEADS), dt, 0.0)
    a = -jnp.exp(alog_ref[...])
    da = dt * a
    tri_r = lax.broadcasted_iota(jnp.int32, (cl, cl), 0)
    tri_c = lax.broadcasted_iota(jnp.int32, (cl, cl), 1)
    causal = tri_r >= tri_c
    ltri = jnp.where(causal, 1.0, 0.0).astype(BF16)
    acs = _dot_exact_rhs(ltri, da)
    acs_t = acs.T
    er = lax.broadcasted_iota(jnp.int32, (LANES, D_INNER), 0)
    ec = lax.broadcasted_iota(jnp.int32, (LANES, D_INNER), 1) // SSD_HEAD_DIM
    expand = jnp.where(er == ec, 1.0, 0.0).astype(BF16)
    dt_full = _dot_exact_lhs(dt, expand)
    acs_full = _dot_exact_lhs(acs, expand)
    last_full = acs_full[cl - 1:cl, :]
    xall = x_ref[0]
    xdt = xall * dt_full
    xdt_end = (xdt * jnp.exp(last_full - acs_full)).astype(BF16)
    xdt_b = xdt.astype(BF16)
    e_acs = jnp.exp(acs_full)
    chunk_decay = jnp.exp(last_full)
    lane_lo = lax.broadcasted_iota(jnp.int32, (1, LANES), 1) < SSD_HEAD_DIM
    ys = []
    for g in range(SSD_GROUPS):
        bg = b_ref[0][:, g * SSD_STATE:(g + 1) * SSD_STATE]
        cg = c_ref[0][:, g * SSD_STATE:(g + 1) * SSD_STATE].astype(BF16)
        cb = lax.dot_general(cg, bg.astype(BF16), (((1,), (1,)), ((), ())), preferred_element_type=F32)
        bg_t = bg.T.astype(BF16)
        for mm in range(pairs_per_group):
            m = g * pairs_per_group + mm
            sl = slice(m * LANES, (m + 1) * LANES)
            yd = None
            for hh in range(2):
                h = 2 * m + hh
                seg = acs[:, h:h + 1] - acs_t[h:h + 1, :]
                decay = jnp.exp(jnp.where(causal, seg, NEG_INF))
                sc = (cb * decay).astype(BF16)
                xm = jnp.where(lane_lo if hh == 0 else ~lane_lo, xdt_b[:, sl], jnp.zeros((), BF16))
                d = jnp.dot(sc, xm, preferred_element_type=F32)
                yd = d if yd is None else yd + d
            st = st_ref[m]
            yo = jnp.dot(cg, st.astype(BF16), preferred_element_type=F32) * e_acs[:, sl]
            cs = jnp.dot(bg_t, xdt_end[:, sl], preferred_element_type=F32)
            st_ref[m] = st * chunk_decay[:, sl] + cs
            ys.append(yd + yo)
    y = jnp.concatenate(ys, axis=1) + xall * d_ref[...]
    zz = z_ref[0]
    y = y * (zz * jax.nn.sigmoid(zz))
    gw = D_INNER // SSD_GROUPS
    outs = []
    for g in range(SSD_GROUPS):
        s = y[:, g * gw:(g + 1) * gw]
        r = lax.rsqrt(jnp.mean(s * s, axis=-1, keepdims=True) + EPS)
        outs.append(s * r)
    y_ref[0] = jnp.concatenate(outs, axis=1) * ng_ref[...]

    @pl.when(c == nc - 1)
    def _():
        for m in range(n_pairs):
            hout_ref[0, m] = st_ref[m].T


def ssd_scan(xact, proj3, dt_bias, a_log, d_full, norm_g, h0, t_valid):
    b, t, _ = xact.shape
    cl = SSD_CHUNK
    nc = t // cl
    gn = SSD_GROUPS * SSD_STATE
    body = functools.partial(_ssd_body, cl=cl, t_valid=t_valid)
    n_pairs = SSD_HEADS // 2
    return pl.pallas_call(
        body,
        grid=(b, nc),
        in_specs=[
            pl.BlockSpec((1, cl, D_INNER), lambda i, c: (i, c, 0)),
            pl.BlockSpec((1, cl, gn), lambda i, c: (i, c, D_INNER // gn)),
            pl.BlockSpec((1, cl, gn), lambda i, c: (i, c, D_INNER // gn + 1)),
            pl.BlockSpec((1, cl, D_INNER), lambda i, c: (i, c, 0)),
            pl.BlockSpec((1, cl, LANES), lambda i, c: (i, c, (D_INNER + CONV_DIM) // LANES)),
            pl.BlockSpec((1, LANES), lambda i, c: (0, 0)),
            pl.BlockSpec((1, LANES), lambda i, c: (0, 0)),
            pl.BlockSpec((1, D_INNER), lambda i, c: (0, 0)),
            pl.BlockSpec((1, D_INNER), lambda i, c: (0, 0)),
            pl.BlockSpec((1, n_pairs, LANES, SSD_STATE), lambda i, c: (i, 0, 0, 0)),
        ],
        out_specs=[
            pl.BlockSpec((1, cl, D_INNER), lambda i, c: (i, c, 0)),
            pl.BlockSpec((1, n_pairs, LANES, SSD_STATE), lambda i, c: (i, 0, 0, 0)),
        ],
        out_shape=[
            jax.ShapeDtypeStruct((b, t, D_INNER), F32),
            jax.ShapeDtypeStruct((b, n_pairs, LANES, SSD_STATE), F32),
        ],
        scratch_shapes=[pltpu.VMEM((n_pairs, SSD_STATE, LANES), F32)],
        compiler_params=_cparams(("arbitrary", "arbitrary")),
        name="ssd_scan",
    )(xact, xact, xact, proj3, proj3, dt_bias, a_log, d_full, norm_g, h0)


def _cmp_exchange(vals, i, j):
    hi = jnp.maximum(vals[i], vals[j])
    lo = jnp.minimum(vals[i], vals[j])
    vals[i], vals[j] = hi, lo


def _bitonic_merge_desc(vals):
    n = len(vals)
    j = n // 2
    while j >= 1:
        for i in range(n):
            if i & j == 0:
                _cmp_exchange(vals, i, i | j)
        j //= 2
    return vals


def _bitonic_sort_desc(vals):
    n = len(vals)
    k = 2
    while k <= n:
        j = k // 2
        while j >= 1:
            for i in range(n):
                l = i ^ j
                if l > i:
                    if i & k == 0:
                        _cmp_exchange(vals, i, l)
                    else:
                        _cmp_exchange(vals, l, i)
            j //= 2
        k *= 2
    return vals


def _merge_top(xs, ys):
    n = len(xs)
    return _bitonic_merge_desc([jnp.maximum(xs[i], ys[n - 1 - i]) for i in range(n)])


def _top16_rows(s):
    slabs = [s[SUBLANES * r:SUBLANES * (r + 1), :] for r in range(s.shape[0] // SUBLANES)]
    assert len(slabs) == PEER_TOPK
    top = _bitonic_sort_desc(slabs)
    shift = 1
    while shift < SUBLANES:
        top = _merge_top(top, [pltpu.roll(x, shift, 0) for x in top])
        shift *= 2
    return top


def _top16_of(groups):
    done = [groups[0]] + [_bitonic_sort_desc(list(gp)) for gp in groups[1:]]
    while len(done) > 1:
        done = [_merge_top(done[i], done[i + 1]) if i + 1 < len(done) else done[i]
                for i in range(0, len(done), 2)]
    return done[0]


def _peer_body(x_ref, g_ref, sh_ref, sc_ref, gt_ref, wq_ref, k1_ref, k2_ref, u_ref, v_ref, og_ref,
               o_ref, ht_ref, s2_ref, f2_ref, th_ref, f1_ref, w_ref, acc_ref, *, tt, eb, final_norm):
    j = pl.program_id(1)
    nj = pl.num_programs(1)
    ncol = tt // LANES
    k = PEER_TOPK

    @pl.when(j == 0)
    def _():
        h = _modulate(x_ref[...], g_ref[...], sh_ref[0], sc_ref[0])
        ht_ref[...] = h.T.astype(BF16)
        acc_ref[...] = jnp.zeros_like(acc_ref)

        def per_head(hd, _):
            r0 = pl.multiple_of(hd * D_KEY, D_KEY)
            qh = jnp.dot(wq_ref[pl.ds(r0, D_KEY), :], ht_ref[...], preferred_element_type=F32)
            s1 = jnp.dot(k1_ref[hd], qh[:D_HALF].astype(BF16), preferred_element_type=F32)
            s2 = jnp.dot(k2_ref[hd], qh[D_HALF:].astype(BF16), preferred_element_type=F32)
            v1 = _top16_rows(s1)
            v2 = _top16_rows(s2)
            width = [k // (a_ + 1) for a_ in range(k)]
            cand = [[v1[a_] + v2[b_] for b_ in range(width[a_])] for a_ in range(k)]
            rest = [c for row in cand[1:] for c in row]
            rest += [jnp.full_like(v1[0], NEG_INF)] * (-len(rest) % k)
            best = _top16_of([cand[0]] + [rest[i:i + k] for i in range(0, len(rest), k)])
            tau = best[k - 1]
            zsum = None
            for b_ in best:
                e = jnp.exp(b_ - best[0])
                zsum = e if zsum is None else zsum + e
            theta = []
            for a_ in range(k):
                th = jnp.full_like(tau, float("inf"))
                for b_ in range(width[a_]):
                    th = jnp.where(cand[a_][b_] >= tau, v2[b_], th)
                theta.append(th)
            inv_z = 1.0 / zsum
            for r in range(N_KEYS // SUBLANES):
                sl = slice(SUBLANES * r, SUBLANES * (r + 1))
                s1r, s2r = s1[sl], s2[sl]
                th = jnp.full_like(s1r, float("inf"))
                for a_ in reversed(range(k)):
                    th = jnp.where(s1r == v1[a_], theta[a_], th)
                f1 = jnp.exp(s1r - v1[0]) * inv_z
                f2 = jnp.exp(s2r - v2[0])
                for cc in range(ncol):
                    cs = slice(cc * LANES, (cc + 1) * LANES)
                    s2_ref[hd, cc, sl, :] = s2r[:, cs]
                    f2_ref[hd, cc, sl, :] = f2[:, cs]
                    th_ref[hd, cc, sl, :] = th[:, cs]
                    f1_ref[hd, cc, sl, :] = f1[:, cs]
            return 0

        lax.fori_loop(0, PEER_HEADS, per_head, 0)

    n_i1 = eb // N_KEYS
    for ii in range(n_i1):
        i1 = j * n_i1 + ii
        for cc in range(ncol):
            w = jnp.zeros((N_KEYS, LANES), F32)
            for hd in range(PEER_HEADS):
                th = th_ref[hd, cc, pl.ds(i1, 1), :]
                f1r = f1_ref[hd, cc, pl.ds(i1, 1), :]
                w = w + jnp.where(s2_ref[hd, cc] >= th, f2_ref[hd, cc], 0.0) * f1r
            w_ref[ii * N_KEYS:(ii + 1) * N_KEYS, cc * LANES:(cc + 1) * LANES] = w
    a = jnp.dot(u_ref[...], ht_ref[...], preferred_element_type=F32)
    act = 0.5 * a * (1.0 + lax.erf(a * (2.0 ** -0.5)))
    p = (w_ref[...] * act).astype(BF16)
    acc_ref[...] += lax.dot_general(p, v_ref[...], (((0,), (0,)), ((), ())),
                                    preferred_element_type=F32)

    @pl.when(j == nj - 1)
    def _():
        y = x_ref[...] + gt_ref[0] * acc_ref[...]
        if final_norm:
            r = lax.rsqrt(jnp.mean(y * y, axis=-1, keepdims=True) + EPS)
            y = (y * r) * og_ref[...]
        o_ref[...] = y


def peer_layer(x, g, sh, sc, gt, wq_t, k1, k2, u, v, out_g, tt, eb, final_norm=False):
    n = x.shape[0]
    nt = n // tt
    ncol = tt // LANES
    nj = N_EXPERTS // eb
    body = functools.partial(_peer_body, tt=tt, eb=eb, final_norm=final_norm)

    def grp(a):
        gg, r, c = a.shape
        tpg = nt // gg
        return pl.BlockSpec((1, r, c), lambda i, j: (i // tpg, 0, 0))

    return pl.pallas_call(
        body,
        grid=(nt, nj),
        in_specs=[
            pl.BlockSpec((tt, D_MODEL), lambda i, j: (i, 0)),
            pl.BlockSpec((1, D_MODEL), lambda i, j: (0, 0)),
            grp(sh), grp(sc), grp(gt),
            pl.BlockSpec(wq_t.shape, lambda i, j: (0, 0)),
            pl.BlockSpec(k1.shape, lambda i, j: (0, 0, 0)),
            pl.BlockSpec(k2.shape, lambda i, j: (0, 0, 0)),
            pl.BlockSpec((eb, D_MODEL), lambda i, j: (j, 0)),
            pl.BlockSpec((eb, D_MODEL), lambda i, j: (j, 0)),
            pl.BlockSpec((1, D_MODEL), lambda i, j: (0, 0)),
        ],
        out_specs=pl.BlockSpec((tt, D_MODEL), lambda i, j: (i, 0)),
        out_shape=jax.ShapeDtypeStruct((n, D_MODEL), F32),
        scratch_shapes=[
            pltpu.VMEM((D_MODEL, tt), BF16),
            pltpu.VMEM((PEER_HEADS, ncol, N_KEYS, LANES), F32),
            pltpu.VMEM((PEER_HEADS, ncol, N_KEYS, LANES), F32),
            pltpu.VMEM((PEER_HEADS, ncol, N_KEYS, LANES), F32),
            pltpu.VMEM((PEER_HEADS, ncol, N_KEYS, LANES), F32),
            pltpu.VMEM((eb, tt), F32),
            pltpu.VMEM((tt, D_MODEL), F32),
        ],
        compiler_params=_cparams(("arbitrary", "arbitrary")),
        name="peer",
    )(x, g, sh, sc, gt, wq_t, k1, k2, u, v, out_g)


def _row_groups(a, bsz, t, tm):
    if t % tm == 0:
        return a[:, None, :]
    assert tm % t == 0
    per_row = jnp.repeat(a, t, axis=0)
    return per_row.reshape(bsz * t // tm, tm, a.shape[1])


def _trunk(x, c_mod, start_pos, past, pool_buf, conv_buf, ssm_state, wts, tm, tt, eb):
    bsz, t, _ = x.shape
    n = bsz * t
    x2 = x.reshape(n, D_MODEL)
    rg = lambda a: _row_groups(a, bsz, t, tm)
    rgp = lambda a: _row_groups(a, bsz, t, tt)

    sh1, sc1, g1, sh2, sc2, g2 = c_mod[0]
    proj = matmul([x2], [wts["ab_w_in"]], tm, pro=(wts["norm_mix_g"][0], rg(sh1), rg(sc1)), name="ab_in")
    proj3 = proj.reshape(bsz, t, -1)
    k_new = proj3[..., D_A + D_B:D_A + 2 * D_B]
    v_new = proj3[..., D_A + 2 * D_B:]
    buf16 = jnp.pad(pool_buf, ((0, 0), (POOL_HALO - POOL_BUF, 0), (0, 0)))
    ya, nb16 = pool_mixer(proj3, buf16, wts["pool_wbd"], wts["pool_scale"], start_pos)
    new_pool = nb16[:, POOL_HALO - POOL_BUF:]
    if past is None:
        yb = attn_prompt(proj3, wts["sb_bias"])
    else:
        cache_k, cache_v, page_table = past
        page = cache_k.shape[2]
        q = proj3[..., D_A:D_A + D_B]
        padr = ((0, 0), (0, page - t), (0, 0))
        bias_rows = jnp.repeat(wts["sb_bias"], t)[:, None]
        yb = attn_sample(q, jnp.pad(k_new, padr), jnp.pad(v_new, padr), cache_k, cache_v, page_table, bias_rows)
    x2 = matmul([ya.reshape(n, D_A), yb.reshape(n, D_B)], [wts["ab_w_out_a"], wts["ab_w_out_b"]], tm,
                res=(x2, rg(g1)), name="ab_out")
    x2 = peer_layer(x2, wts["norm_ffn_g"][0], rgp(sh2), rgp(sc2), rgp(g2), wts["peer_wq_t"][0],
                    wts["peer_k1"][0], wts["peer_k2"][0], wts["peer_u"][0], wts["peer_v"][0],
                    wts["norm_out_g"], tt, eb)

    sh1, sc1, g1, sh2, sc2, g2 = c_mod[1]
    proj = matmul([x2], [wts["ssd_w_in"]], tm, pro=(wts["norm_mix_g"][1], rg(sh1), rg(sc1)), name="ssd_in")
    proj3 = proj.reshape(bsz, t, -1)
    buf8 = jnp.pad(conv_buf, ((0, 0), (CONV_HALO - (SSD_CONV - 1), 0), (0, 0)))
    xact, nc8 = conv_silu(proj3, buf8, wts["ssd_conv_w"], wts["ssd_conv_b"])
    new_conv = nc8[:, CONV_HALO - (SSD_CONV - 1):]
    tp = -(-t // SSD_CHUNK) * SSD_CHUNK
    if tp != t:
        padt = ((0, 0), (0, tp - t), (0, 0))
        xact = jnp.pad(xact, padt)
        proj3 = jnp.pad(proj3, padt)
    h0 = ssm_state.reshape(bsz, SSD_HEADS // 2, LANES, SSD_STATE)
    ynorm, hlast = ssd_scan(xact, proj3, wts["ssd_dt_bias"], wts["ssd_a_log"], wts["ssd_d_full"],
                            wts["ssd_norm_g"], h0, t)
    new_ssm = hlast.reshape(bsz, SSD_HEADS, SSD_HEAD_DIM, SSD_STATE)
    ynorm = ynorm[:, :t].reshape(n, D_INNER)
    x2 = matmul([ynorm], [wts["ssd_w_out"]], tm, res=(x2, rg(g1)), name="ssd_out")
    x2 = peer_layer(x2, wts["norm_ffn_g"][1], rgp(sh2), rgp(sc2), rgp(g2), wts["peer_wq_t"][1],
                    wts["peer_k1"][1], wts["peer_k2"][1], wts["peer_u"][1], wts["peer_v"][1],
                    wts["norm_out_g"], tt, eb, final_norm=True)
    return (x2.reshape(bsz, t, D_MODEL), k_new.reshape(bsz, t, SB_HEADS, SB_HEAD_DIM)[None],
            v_new.reshape(bsz, t, SB_HEADS, SB_HEAD_DIM)[None], new_pool[None], new_conv[None], new_ssm[None])


def kernel(x_prompt, x_sample, cache_sb_k, cache_sb_v, state_pool, state_conv, state_ssm, page_table,
           c_prompt, c_sample, ada_w, ada_b, norm_mix_g, norm_ffn_g, norm_out_g,
           ab_w_in, ab_w_out, pool_w, pool_scale, sb_bias,
           ssd_w_in, ssd_conv_w, ssd_conv_b, ssd_dt_bias, ssd_a_log, ssd_d, ssd_norm_g, ssd_w_out,
           peer_wq, peer_k1, peer_k2, peer_u, peer_v):
    bsz, seq, _ = x_prompt.shape
    dbsz, dseq, _ = x_sample.shape
    depth = ada_w.shape[0]
    assert depth == 2 and ab_w_in.shape[0] == 1 and ssd_w_in.shape[0] == 1

    ssd_pad = (-D_SSD_IN) % LANES
    pad_lanes = lambda a: jnp.pad(a, ((0, 0), (0, LANES - a.shape[1])))
    wts = {
        "norm_mix_g": [norm_mix_g[l][None] for l in range(depth)],
        "norm_ffn_g": [norm_ffn_g[l][None] for l in range(depth)],
        "norm_out_g": norm_out_g[None],
        "ab_w_in": ab_w_in[0].astype(BF16),
        "ab_w_out_a": ab_w_out[0, :D_A].astype(BF16),
        "ab_w_out_b": ab_w_out[0, D_A:].astype(BF16),
        "pool_wbd": jax.scipy.linalg.block_diag(*[pool_w[0, g] for g in range(len(POOL_WINDOWS))]).astype(BF16),
        "pool_scale": pool_scale[0][None],
        "sb_bias": sb_bias[0],
        "ssd_w_in": jnp.pad(ssd_w_in[0], ((0, 0), (0, ssd_pad))).astype(BF16),
        "ssd_conv_w": ssd_conv_w[0],
        "ssd_conv_b": ssd_conv_b[0][None],
        "ssd_dt_bias": pad_lanes(ssd_dt_bias[0][None]),
        "ssd_a_log": pad_lanes(ssd_a_log[0][None]),
        "ssd_d_full": jnp.repeat(ssd_d[0], SSD_HEAD_DIM)[None],
        "ssd_norm_g": ssd_norm_g[0][None],
        "ssd_w_out": ssd_w_out[0].astype(BF16),
        "peer_wq_t": [peer_wq[l].T.astype(BF16) for l in range(depth)],
        "peer_k1": [peer_k1[l].astype(BF16) for l in range(depth)],
        "peer_k2": [peer_k2[l].astype(BF16) for l in range(depth)],
        "peer_u": [peer_u[l].astype(BF16) for l in range(depth)],
        "peer_v": [peer_v[l].astype(BF16) for l in range(depth)],
    }

    c_all = jnp.concatenate([c_prompt, c_sample], axis=0)
    mods_p, mods_s = [], []
    for l in range(depth):
        mod = matmul([c_all], [ada_w[l].astype(BF16)], c_all.shape[0], bias=ada_b[l][None], name="ada")
        parts = jnp.split(mod, 6, axis=-1)
        mods_p.append([p[:bsz] for p in parts])
        mods_s.append([p[bsz:] for p in parts])

    zeros = lambda shape: jnp.zeros(shape, F32)
    out_p = _trunk(x_prompt, mods_p, 0, None, zeros((bsz, POOL_BUF, D_A)), zeros((bsz, SSD_CONV - 1, CONV_DIM)),
                   zeros((bsz, SSD_HEADS, SSD_HEAD_DIM, SSD_STATE)), wts, tm=256, tt=512, eb=512)
    n_pool, page = cache_sb_k.shape[1], cache_sb_k.shape[2]
    past = (cache_sb_k.reshape(1, n_pool, page, D_B), cache_sb_v.reshape(1, n_pool, page, D_B), page_table)
    past_len = page_table.shape[1] * page
    out_s = _trunk(x_sample, mods_s, past_len, past, state_pool[0], state_conv[0], state_ssm[0], wts,
                   tm=dbsz * dseq, tt=dbsz * dseq, eb=512)
    y_p, k_p, v_p, pool_p, conv_p, ssm_p = out_p
    y_s, k_s, v_s, pool_s, conv_s, ssm_s = out_s
    return (y_p, y_s, k_p, v_p, k_s, v_s, pool_p, pool_s, conv_p, conv_s, ssm_p, ssm_s)
```

```python
import functools
import math

import jax
import jax.numpy as jnp
from jax import lax
from jax.experimental import pallas as pl
from jax.experimental.pallas import tpu as pltpu

F32 = jnp.float32
BF16 = jnp.bfloat16

D_MODEL = 1024
POOL_WINDOWS = (2, 4, 8, 16)
D_A = 256
POOL_GC = 64
POOL_BUF = 15
SB_HEAD_DIM = 64
D_B = 768
SB_HEADS = 12
D_INNER = 2048
SSD_HEAD_DIM = 64
SSD_HEADS = 32
SSD_GROUPS = 4
SSD_STATE = 128
SSD_CONV = 4
SSD_CHUNK = 128
CONV_DIM = 3072
D_SSD_IN = D_INNER + CONV_DIM + SSD_HEADS
PEER_HEADS = 8
N_KEYS = 128
N_EXPERTS = N_KEYS * N_KEYS
PEER_TOPK = 16
D_KEY = 256
D_HALF = 128
EPS = 1e-6

LANES = 128
SUBLANES = 8
VMEM_LIMIT = 56 * 1024 * 1024

NEG_INF = float("-inf")


def _cparams(sem):
    return pltpu.CompilerParams(dimension_semantics=sem, vmem_limit_bytes=VMEM_LIMIT)


def _modulate(x, g, shift, scale):
    r = lax.rsqrt(jnp.mean(x * x, axis=-1, keepdims=True) + EPS)
    return (x * r) * g * (1.0 + scale) + shift


def _split3(x):
    hi = x.astype(BF16)
    r1 = x - hi.astype(F32)
    mid = r1.astype(BF16)
    lo = (r1 - mid.astype(F32)).astype(BF16)
    return hi, mid, lo


def _dot_exact_rhs(a_bf16, x):
    hi, mid, lo = _split3(x)
    return (jnp.dot(a_bf16, hi, preferred_element_type=F32)
            + jnp.dot(a_bf16, mid, preferred_element_type=F32)
            + jnp.dot(a_bf16, lo, preferred_element_type=F32))


def _dot_exact_lhs(x, a_bf16):
    hi, mid, lo = _split3(x)
    return (jnp.dot(hi, a_bf16, preferred_element_type=F32)
            + jnp.dot(mid, a_bf16, preferred_element_type=F32)
            + jnp.dot(lo, a_bf16, preferred_element_type=F32))


def _mm_body(*refs, n_x, has_pro, has_bias, has_res):
    it = iter(refs)
    xs = [next(it) for _ in range(n_x)]
    ws = [next(it) for _ in range(n_x)]
    if has_pro:
        g_ref, sh_ref, sc_ref = next(it), next(it), next(it)
    if has_bias:
        b_ref = next(it)
    if has_res:
        r_ref, gt_ref = next(it), next(it)
    o_ref = next(it)
    acc = None
    for x_ref, w_ref in zip(xs, ws):
        x = x_ref[...]
        if has_pro:
            x = _modulate(x, g_ref[...], sh_ref[0], sc_ref[0])
        d = jnp.dot(x.astype(BF16), w_ref[...], preferred_element_type=F32)
        acc = d if acc is None else acc + d
    if has_bias:
        acc = acc + b_ref[...]
    if has_res:
        acc = r_ref[...] + gt_ref[0] * acc
    o_ref[...] = acc


def _group_spec(a, n_tiles):
    g, r, c = a.shape
    tiles_per_group = n_tiles // g
    return pl.BlockSpec((1, r, c), lambda i: (i // tiles_per_group, 0, 0))


def matmul(xs, ws, tm, pro=None, bias=None, res=None, name="mm"):
    m = xs[0].shape[0]
    n = ws[0].shape[1]
    nt = m // tm
    assert nt * tm == m
    in_specs = [pl.BlockSpec((tm, x.shape[1]), lambda i: (i, 0)) for x in xs]
    in_specs += [pl.BlockSpec(w.shape, lambda i: (0, 0)) for w in ws]
    args = list(xs) + list(ws)
    if pro is not None:
        g, sh, sc = pro
        in_specs += [pl.BlockSpec(g.shape, lambda i: (0, 0)), _group_spec(sh, nt), _group_spec(sc, nt)]
        args += [g, sh, sc]
    if bias is not None:
        in_specs.append(pl.BlockSpec(bias.shape, lambda i: (0, 0)))
        args.append(bias)
    if res is not None:
        r, gt = res
        in_specs += [pl.BlockSpec((tm, n), lambda i: (i, 0)), _group_spec(gt, nt)]
        args += [r, gt]
    body = functools.partial(_mm_body, n_x=len(xs), has_pro=pro is not None,
                             has_bias=bias is not None, has_res=res is not None)
    return pl.pallas_call(
        body,
        grid=(nt,),
        in_specs=in_specs,
        out_specs=pl.BlockSpec((tm, n), lambda i: (i, 0)),
        out_shape=jax.ShapeDtypeStruct((m, n), F32),
        compiler_params=_cparams(("arbitrary",)),
        name=name,
    )(*args)


POOL_HALO = 16


def _pool_body(cur_ref, prev_ref, buf_ref, wbd_ref, ps_ref, y_ref, nb_ref, win_ref, *, ch, start_pos):
    c = pl.program_id(1)
    cur = cur_ref[0]
    if ch >= POOL_HALO:
        tail = jnp.where(c == 0, buf_ref[0], prev_ref[0][ch - POOL_HALO:ch])
    else:
        tail = buf_ref[0]
    win_ref[0:POOL_HALO, :] = tail
    win_ref[POOL_HALO:POOL_HALO + ch, :] = cur

    def shifted(k):
        return win_ref[POOL_HALO - k:POOL_HALO - k + ch, :]

    pos = start_pos + c * ch + lax.broadcasted_iota(jnp.int32, (ch, 1), 0)
    grp = lax.broadcasted_iota(jnp.int32, (1, D_A), 1) // POOL_GC
    s = cur
    mean = jnp.zeros_like(cur)
    k = 1
    for g, w in enumerate(POOL_WINDOWS):
        while k < w:
            s = s + shifted(k)
            k += 1
        cnt = jnp.minimum(pos + 1, w).astype(F32)
        mean = jnp.where(grp == g, s / cnt, mean)
    diff = mean - cur
    y = jnp.dot(diff.astype(BF16), wbd_ref[...], preferred_element_type=F32) * ps_ref[...]
    y_ref[0] = y
    nb_ref[0] = win_ref[ch:ch + POOL_HALO, :]


def pool_mixer(proj3, buf16, wbd, pscale, start_pos):
    b, t, _ = proj3.shape
    ch = min(128, t)
    nc = t // ch
    body = functools.partial(_pool_body, ch=ch, start_pos=start_pos)
    return pl.pallas_call(
        body,
        grid=(b, nc),
        in_specs=[
            pl.BlockSpec((1, ch, D_A), lambda i, c: (i, c, 0)),
            pl.BlockSpec((1, ch, D_A), lambda i, c: (i, jnp.maximum(c - 1, 0), 0)),
            pl.BlockSpec((1, POOL_HALO, D_A), lambda i, c: (i, 0, 0)),
            pl.BlockSpec((D_A, D_A), lambda i, c: (0, 0)),
            pl.BlockSpec((1, D_A), lambda i, c: (0, 0)),
        ],
        out_specs=[
            pl.BlockSpec((1, ch, D_A), lambda i, c: (i, c, 0)),
            pl.BlockSpec((1, POOL_HALO, D_A), lambda i, c: (i, 0, 0)),
        ],
        out_shape=[
            jax.ShapeDtypeStruct((b, t, D_A), F32),
            jax.ShapeDtypeStruct((b, POOL_HALO, D_A), F32),
        ],
        scratch_shapes=[pltpu.VMEM((POOL_HALO + ch, D_A), F32)],
        compiler_params=_cparams(("arbitrary", "arbitrary")),
        name="pool",
    )(proj3, proj3, buf16, wbd, pscale)


SAMPLE_PAGES_PER_STEP = 8


def _cum_matrix(tk, n_ones, ones_lo, ones_hi):
    r = lax.broadcasted_iota(jnp.int32, (tk, tk + n_ones), 0)
    c = lax.broadcasted_iota(jnp.int32, (tk, tk + n_ones), 1)
    ones = (c >= tk + ones_lo) & (c < tk + ones_hi)
    return jnp.where(ones | ((c < tk) & (r > c)), 1.0, 0.0).astype(BF16)


def _sb_block(z, valid, cum):
    tk = z.shape[1]
    l1p = jnp.log1p(jnp.exp(-jnp.abs(z)))
    log_beta = jnp.minimum(z, 0.0) - l1p
    log_keep = jnp.minimum(-z, 0.0) - l1p
    if valid is not None:
        log_keep = jnp.where(valid, log_keep, 0.0)
    hi = log_keep.astype(BF16)
    lo = (log_keep - hi.astype(F32)).astype(BF16)
    cs = (jnp.dot(hi, cum, preferred_element_type=F32)
          + jnp.dot(lo, cum, preferred_element_type=F32))
    w = jnp.exp(log_beta + cs[:, :tk])
    if valid is not None:
        w = jnp.where(valid, w, 0.0)
    return w, cs[:, tk:]


def _attn_prompt_body(bias_ref, q_ref, k_ref, v_ref, o_ref, acc_ref, car_ref, *, tq, tk):
    g = pl.program_id(1)
    i = pl.program_id(2)
    scale = SB_HEAD_DIM ** -0.5
    q = q_ref[0] * scale
    lane = lax.broadcasted_iota(jnp.int32, (1, LANES), 1)
    head_mask = [lane < SB_HEAD_DIM, lane >= SB_HEAD_DIM]
    qm = [jnp.where(m, q, 0.0).astype(BF16) for m in head_mask]
    bias = [bias_ref[2 * g], bias_ref[2 * g + 1]]
    cum = [_cum_matrix(tk, LANES, 0, SB_HEAD_DIM), _cum_matrix(tk, LANES, SB_HEAD_DIM, LANES)]
    acc_ref[...] = jnp.zeros_like(acc_ref)
    car_ref[...] = jnp.zeros_like(car_ref)
    qpos = i * tq + lax.broadcasted_iota(jnp.int32, (tq, tk), 0)
    kiota = lax.broadcasted_iota(jnp.int32, (tq, tk), 1)
    blocks_per_step = tq // tk

    def one_block(k0, masked):
        kb = k_ref[0, pl.ds(k0, tk), :].astype(BF16)
        vb = v_ref[0, pl.ds(k0, tk), :]
        valid = ((k0 + kiota) < qpos) if masked else None
        ws, rs = [], None
        for h in range(2):
            z = lax.dot_general(qm[h], kb, (((1,), (1,)), ((), ())), preferred_element_type=F32) + bias[h]
            w, r = _sb_block(z, valid, cum[h])
            ws.append(w.astype(BF16))
            rs = r if rs is None else rs + r
        vm = jnp.concatenate([jnp.where(head_mask[h], vb, 0.0).astype(BF16) for h in range(2)], axis=0)
        o = jnp.dot(jnp.concatenate(ws, axis=1), vm, preferred_element_type=F32)
        return o, rs

    def step(first_block, masked):
        res = [one_block(pl.multiple_of(first_block * tk + r * tk, tk), masked)
               for r in range(blocks_per_step)]
        for o, rs in reversed(res):
            car = car_ref[...]
            acc_ref[...] += jnp.exp(car) * o
            car_ref[...] = car + rs

    step(i * blocks_per_step, True)

    def body(jj, _):
        step((i - 1 - jj) * blocks_per_step, False)
        return 0

    lax.fori_loop(0, i, body, 0)
    o_ref[0] = acc_ref[...]


def attn_prompt(proj3, sb_bias, tq=256, tk=128):
    b, t, _ = proj3.shape
    tq = min(tq, t)
    tk = min(tk, tq)
    n_pairs = SB_HEADS // 2
    qo = D_A // LANES
    body = functools.partial(_attn_prompt_body, tq=tq, tk=tk)
    return pl.pallas_call(
        body,
        grid=(b, n_pairs, t // tq),
        in_specs=[
            pl.BlockSpec(memory_space=pltpu.SMEM),
            pl.BlockSpec((1, tq, LANES), lambda bi, g, i: (bi, i, qo + g)),
            pl.BlockSpec((1, t, LANES), lambda bi, g, i: (bi, 0, qo + n_pairs + g)),
            pl.BlockSpec((1, t, LANES), lambda bi, g, i: (bi, 0, qo + 2 * n_pairs + g)),
        ],
        out_specs=pl.BlockSpec((1, tq, LANES), lambda bi, g, i: (bi, i, g)),
        scratch_shapes=[pltpu.VMEM((tq, LANES), F32), pltpu.VMEM((tq, LANES), F32)],
        out_shape=jax.ShapeDtypeStruct((b, t, D_B), F32),
        compiler_params=_cparams(("arbitrary", "arbitrary", "arbitrary")),
        name="attn_prompt",
    )(sb_bias, proj3, proj3, proj3)


def _attn_sample_body(pt_ref, q_ref, kn_ref, vn_ref, *rest, t_new, page, ppb):
    kp_refs, vp_refs = rest[:ppb], rest[ppb:2 * ppb]
    bias_ref, o_ref, qbd_ref, acc_ref, car_ref = rest[2 * ppb:]
    p = pl.program_id(1)
    n_steps = pl.num_programs(1)
    rows = SB_HEADS * t_new
    row_head = lax.broadcasted_iota(jnp.int32, (rows, D_B), 0) // t_new
    col_head = lax.broadcasted_iota(jnp.int32, (rows, D_B), 1) // SB_HEAD_DIM
    blockdiag = row_head == col_head
    cum = _cum_matrix(page, LANES, 0, LANES)

    def block(kb, vb, valid):
        z = lax.dot_general(qbd_ref[...], kb.astype(BF16), (((1,), (1,)), ((), ())),
                            preferred_element_type=F32) + bias_ref[...]
        w, rs = _sb_block(z, valid, cum)
        return jnp.dot(w.astype(BF16), vb.astype(BF16), preferred_element_type=F32), rs

    def accumulate(res):
        for o, rs in res:
            car = car_ref[...]
            e = jnp.exp(car)
            acc_ref[...] += jnp.concatenate([e] * (D_B // LANES), axis=1) * o
            car_ref[...] = car + rs

    @pl.when(p == 0)
    def _():
        scale = SB_HEAD_DIM ** -0.5
        q = q_ref[0] * scale
        qt = jnp.concatenate([q] * SB_HEADS, axis=0)
        qbd_ref[...] = jnp.where(blockdiag, qt, 0.0).astype(BF16)
        acc_ref[...] = jnp.zeros_like(acc_ref)
        car_ref[...] = jnp.zeros_like(car_ref)
        qi = lax.broadcasted_iota(jnp.int32, (rows, page), 0) % t_new
        ki = lax.broadcasted_iota(jnp.int32, (rows, page), 1)
        accumulate([block(kn_ref[0], vn_ref[0], ki < qi)])

    @pl.when(p > 0)
    def _():
        accumulate([block(kp_refs[r][0, 0], vp_refs[r][0, 0], None) for r in range(ppb)])

    @pl.when(p == n_steps - 1)
    def _():
        a = jnp.where(blockdiag, acc_ref[...], 0.0)
        out = a[0:t_new]
        for h in range(1, SB_HEADS):
            out = out + a[h * t_new:(h + 1) * t_new]
        o_ref[0] = out


def attn_sample(q, k_new, v_new, cache_k, cache_v, page_table, bias_rows):
    b, t_new, _ = q.shape
    page = cache_k.shape[2]
    n_pages = page_table.shape[1]
    rows = SB_HEADS * t_new
    ppb = math.gcd(n_pages, SAMPLE_PAGES_PER_STEP)

    def page_map(r):
        return lambda bi, p, pt: (0, pt[bi, n_pages - 1 - ((jnp.maximum(p, 1) - 1) * ppb + r)], 0, 0)

    page_specs = [pl.BlockSpec((1, 1, page, D_B), page_map(r)) for r in range(ppb)]
    grid_spec = pltpu.PrefetchScalarGridSpec(
        num_scalar_prefetch=1,
        grid=(b, n_pages // ppb + 1),
        in_specs=[
            pl.BlockSpec((1, t_new, D_B), lambda bi, p, pt: (bi, 0, 0)),
            pl.BlockSpec((1, page, D_B), lambda bi, p, pt: (bi, 0, 0)),
            pl.BlockSpec((1, page, D_B), lambda bi, p, pt: (bi, 0, 0)),
        ] + page_specs + page_specs + [
            pl.BlockSpec((rows, 1), lambda bi, p, pt: (0, 0)),
        ],
        out_specs=pl.BlockSpec((1, t_new, D_B), lambda bi, p, pt: (bi, 0, 0)),
        scratch_shapes=[pltpu.VMEM((rows, D_B), BF16), pltpu.VMEM((rows, D_B), F32),
                        pltpu.VMEM((rows, LANES), F32)],
    )
    body = functools.partial(_attn_sample_body, t_new=t_new, page=page, ppb=ppb)
    return pl.pallas_call(
        body,
        grid_spec=grid_spec,
        out_shape=jax.ShapeDtypeStruct((b, t_new, D_B), F32),
        compiler_params=_cparams(("arbitrary", "arbitrary")),
        name="attn_sample",
    )(page_table, q, k_new, v_new, *([cache_k] * ppb), *([cache_v] * ppb), bias_rows)


CONV_HALO = 8
CONV_CB = 1024


def _conv_body(cur_ref, halo_ref, buf_ref, w_ref, b_ref, y_ref, nb_ref, win_ref, *, ch):
    c = pl.program_id(2)
    tail = jnp.where(c == 0, buf_ref[0], halo_ref[0])
    win_ref[0:CONV_HALO, :] = tail
    win_ref[CONV_HALO:CONV_HALO + ch, :] = cur_ref[0]
    w = w_ref[...]
    y = b_ref[...]
    for i in range(SSD_CONV):
        off = CONV_HALO - (SSD_CONV - 1) + i
        y = y + win_ref[off:off + ch, :] * w[i:i + 1, :]
    y_ref[0] = y * jax.nn.sigmoid(y)
    nb_ref[0] = win_ref[ch:ch + CONV_HALO, :]


def conv_silu(proj3, buf8, conv_w, conv_b):
    b, t, _ = proj3.shape
    ch = min(128, t)
    nc = t // ch
    col0 = D_INNER // CONV_CB
    ncb = CONV_DIM // CONV_CB
    hb = ch // CONV_HALO
    body = functools.partial(_conv_body, ch=ch)
    return pl.pallas_call(
        body,
        grid=(b, ncb, nc),
        in_specs=[
            pl.BlockSpec((1, ch, CONV_CB), lambda i, j, c: (i, c, col0 + j)),
            pl.BlockSpec((1, CONV_HALO, CONV_CB), lambda i, j, c: (i, jnp.maximum(c * hb - 1, 0), col0 + j)),
            pl.BlockSpec((1, CONV_HALO, CONV_CB), lambda i, j, c: (i, 0, j)),
            pl.BlockSpec((SSD_CONV, CONV_CB), lambda i, j, c: (0, j)),
            pl.BlockSpec((1, CONV_CB), lambda i, j, c: (0, j)),
        ],
        out_specs=[
            pl.BlockSpec((1, ch, CONV_CB), lambda i, j, c: (i, c, j)),
            pl.BlockSpec((1, CONV_HALO, CONV_CB), lambda i, j, c: (i, 0, j)),
        ],
        out_shape=[
            jax.ShapeDtypeStruct((b, t, CONV_DIM), F32),
            jax.ShapeDtypeStruct((b, CONV_HALO, CONV_DIM), F32),
        ],
        scratch_shapes=[pltpu.VMEM((CONV_HALO + ch, CONV_CB), F32)],
        compiler_params=_cparams(("arbitrary", "arbitrary", "arbitrary")),
        name="conv",
    )(proj3, proj3, buf8, conv_w, conv_b)


def _ssd_body(x_ref, b_ref, c_ref, z_ref, dt_ref, dtb_ref, alog_ref, d_ref, ng_ref, h0_ref,
              y_ref, hout_ref, st_ref, *, cl, t_valid):
    c = pl.program_id(1)
    nc = pl.num_programs(1)
    n_pairs = SSD_HEADS // 2
    pairs_per_group = n_pairs // SSD_GROUPS

    @pl.when(c == 0)
    def _():
        for m in range(n_pairs):
            st_ref[m] = h0_ref[0, m].T

    row = lax.broadcasted_iota(jnp.int32, (cl, LANES), 0)
    lane = lax.broadcasted_iota(jnp.int32, (cl, LANES), 1)
    dt = jax.nn.softplus(dt_ref[0] + dtb_ref[...])
    dt = jnp.where((c * cl + row < t_valid) & (lane < SSD_HEADS), dt, 0.0)
    a = -jnp.exp(alog_ref[...])
    da = dt * a
    tri_r = lax.broadcasted_iota(jnp.int32, (cl, cl), 0)
    tri_c = lax.broadcasted_iota(jnp.int32, (cl, cl), 1)
    causal = tri_r >= tri_c
    ltri = jnp.where(causal, 1.0, 0.0).astype(BF16)
    acs = _dot_exact_rhs(ltri, da)
    acs_t = acs.T
    er = lax.broadcasted_iota(jnp.int32, (LANES, D_INNER), 0)
    ec = lax.broadcasted_iota(jnp.int32, (LANES, D_INNER), 1) // SSD_HEAD_DIM
    expand = jnp.where(er == ec, 1.0, 0.0).astype(BF16)
    dt_full = _dot_exact_lhs(dt, expand)
    acs_full = _dot_exact_lhs(acs, expand)
    last_full = acs_full[cl - 1:cl, :]
    xall = x_ref[0]
    xdt = xall * dt_full
    xdt_end = (xdt * jnp.exp(last_full - acs_full)).astype(BF16)
    xdt_b = xdt.astype(BF16)
    e_acs = jnp.exp(acs_full)
    chunk_decay = jnp.exp(last_full)
    lane_lo = lax.broadcasted_iota(jnp.int32, (1, LANES), 1) < SSD_HEAD_DIM
    ys = []
    for g in range(SSD_GROUPS):
        bg = b_ref[0][:, g * SSD_STATE:(g + 1) * SSD_STATE]
        cg = c_ref[0][:, g * SSD_STATE:(g + 1) * SSD_STATE].astype(BF16)
        cb = lax.dot_general(cg, bg.astype(BF16), (((1,), (1,)), ((), ())), preferred_element_type=F32)
        bg_t = bg.T.astype(BF16)
        for mm in range(pairs_per_group):
            m = g * pairs_per_group + mm
            sl = slice(m * LANES, (m + 1) * LANES)
            yd = None
            for hh in range(2):
                h = 2 * m + hh
                seg = acs[:, h:h + 1] - acs_t[h:h + 1, :]
                decay = jnp.exp(jnp.where(causal, seg, NEG_INF))
                sc = (cb * decay).astype(BF16)
                xm = jnp.where(lane_lo if hh == 0 else ~lane_lo, xdt_b[:, sl], jnp.zeros((), BF16))
                d = jnp.dot(sc, xm, preferred_element_type=F32)
                yd = d if yd is None else yd + d
            st = st_ref[m]
            yo = jnp.dot(cg, st.astype(BF16), preferred_element_type=F32) * e_acs[:, sl]
            cs = jnp.dot(bg_t, xdt_end[:, sl], preferred_element_type=F32)
            st_ref[m] = st * chunk_decay[:, sl] + cs
            ys.append(yd + yo)
    y = jnp.concatenate(ys, axis=1) + xall * d_ref[...]
    zz = z_ref[0]
    y = y * (zz * jax.nn.sigmoid(zz))
    gw = D_INNER // SSD_GROUPS
    outs = []
    for g in range(SSD_GROUPS):
        s = y[:, g * gw:(g + 1) * gw]
        r = lax.rsqrt(jnp.mean(s * s, axis=-1, keepdims=True) + EPS)
        outs.append(s * r)
    y_ref[0] = jnp.concatenate(outs, axis=1) * ng_ref[...]

    @pl.when(c == nc - 1)
    def _():
        for m in range(n_pairs):
            hout_ref[0, m] = st_ref[m].T


def ssd_scan(xact, proj3, dt_bias, a_log, d_full, norm_g, h0, t_valid):
    b, t, _ = xact.shape
    cl = SSD_CHUNK
    nc = t // cl
    gn = SSD_GROUPS * SSD_STATE
    body = functools.partial(_ssd_body, cl=cl, t_valid=t_valid)
    n_pairs = SSD_HEADS // 2
    return pl.pallas_call(
        body,
        grid=(b, nc),
        in_specs=[
            pl.BlockSpec((1, cl, D_INNER), lambda i, c: (i, c, 0)),
            pl.BlockSpec((1, cl, gn), lambda i, c: (i, c, D_INNER // gn)),
            pl.BlockSpec((1, cl, gn), lambda i, c: (i, c, D_INNER // gn + 1)),
            pl.BlockSpec((1, cl, D_INNER), lambda i, c: (i, c, 0)),
            pl.BlockSpec((1, cl, LANES), lambda i, c: (i, c, (D_INNER + CONV_DIM) // LANES)),
            pl.BlockSpec((1, LANES), lambda i, c: (0, 0)),
            pl.BlockSpec((1, LANES), lambda i, c: (0, 0)),
            pl.BlockSpec((1, D_INNER), lambda i, c: (0, 0)),
            pl.BlockSpec((1, D_INNER), lambda i, c: (0, 0)),
            pl.BlockSpec((1, n_pairs, LANES, SSD_STATE), lambda i, c: (i, 0, 0, 0)),
        ],
        out_specs=[
            pl.BlockSpec((1, cl, D_INNER), lambda i, c: (i, c, 0)),
            pl.BlockSpec((1, n_pairs, LANES, SSD_STATE), lambda i, c: (i, 0, 0, 0)),
        ],
        out_shape=[
            jax.ShapeDtypeStruct((b, t, D_INNER), F32),
            jax.ShapeDtypeStruct((b, n_pairs, LANES, SSD_STATE), F32),
        ],
        scratch_shapes=[pltpu.VMEM((n_pairs, SSD_STATE, LANES), F32)],
        compiler_params=_cparams(("arbitrary", "arbitrary")),
        name="ssd_scan",
    )(xact, xact, xact, proj3, proj3, dt_bias, a_log, d_full, norm_g, h0)


def _cmp_exchange(vals, i, j):
    hi = jnp.maximum(vals[i], vals[j])
    lo = jnp.minimum(vals[i], vals[j])
    vals[i], vals[j] = hi, lo


def _bitonic_merge_desc(vals):
    n = len(vals)
    j = n // 2
    while j >= 1:
        for i in range(n):
            if i & j == 0:
                _cmp_exchange(vals, i, i | j)
        j //= 2
    return vals


def _bitonic_sort_desc(vals):
    n = len(vals)
    k = 2
    while k <= n:
        j = k // 2
        while j >= 1:
            for i in range(n):
                l = i ^ j
                if l > i:
                    if i & k == 0:
                        _cmp_exchange(vals, i, l)
                    else:
                        _cmp_exchange(vals, l, i)
            j //= 2
        k *= 2
    return vals


def _merge_top(xs, ys):
    n = len(xs)
    return _bitonic_merge_desc([jnp.maximum(xs[i], ys[n - 1 - i]) for i in range(n)])


def _top16_rows(s):
    slabs = [s[SUBLANES * r:SUBLANES * (r + 1), :] for r in range(s.shape[0] // SUBLANES)]
    assert len(slabs) == PEER_TOPK
    top = _bitonic_sort_desc(slabs)
    shift = 1
    while shift < SUBLANES:
        top = _merge_top(top, [pltpu.roll(x, shift, 0) for x in top])
        shift *= 2
    return top


def _top16_of(groups):
    done = [groups[0]] + [_bitonic_sort_desc(list(gp)) for gp in groups[1:]]
    while len(done) > 1:
        done = [_merge_top(done[i], done[i + 1]) if i + 1 < len(done) else done[i]
                for i in range(0, len(done), 2)]
    return done[0]


def _peer_body(x_ref, g_ref, sh_ref, sc_ref, gt_ref, wq_ref, k1_ref, k2_ref, u_ref, v_ref, og_ref,
               o_ref, ht_ref, s2_ref, f2_ref, th_ref, f1_ref, w_ref, acc_ref, *, tt, eb, final_norm):
    j = pl.program_id(1)
    nj = pl.num_programs(1)
    ncol = tt // LANES
    k = PEER_TOPK

    @pl.when(j == 0)
    def _():
        h = _modulate(x_ref[...], g_ref[...], sh_ref[0], sc_ref[0])
        ht_ref[...] = h.T.astype(BF16)
        acc_ref[...] = jnp.zeros_like(acc_ref)

        def per_head(hd, _):
            r0 = pl.multiple_of(hd * D_KEY, D_KEY)
            qh = jnp.dot(wq_ref[pl.ds(r0, D_KEY), :], ht_ref[...], preferred_element_type=F32)
            s1 = jnp.dot(k1_ref[hd], qh[:D_HALF].astype(BF16), preferred_element_type=F32)
            s2 = jnp.dot(k2_ref[hd], qh[D_HALF:].astype(BF16), preferred_element_type=F32)
            v1 = _top16_rows(s1)
            v2 = _top16_rows(s2)
            width = [k // (a_ + 1) for a_ in range(k)]
            cand = [[v1[a_] + v2[b_] for b_ in range(width[a_])] for a_ in range(k)]
            rest = [c for row in cand[1:] for c in row]
            rest += [jnp.full_like(v1[0], NEG_INF)] * (-len(rest) % k)
            best = _top16_of([cand[0]] + [rest[i:i + k] for i in range(0, len(rest), k)])
            tau = best[k - 1]
            zsum = None
            for b_ in best:
                e = jnp.exp(b_ - best[0])
                zsum = e if zsum is None else zsum + e
            theta = []
            for a_ in range(k):
                th = jnp.full_like(tau, float("inf"))
                for b_ in range(width[a_]):
                    th = jnp.where(cand[a_][b_] >= tau, v2[b_], th)
                theta.append(th)
            inv_z = 1.0 / zsum
            for r in range(N_KEYS // SUBLANES):
                sl = slice(SUBLANES * r, SUBLANES * (r + 1))
                s1r, s2r = s1[sl], s2[sl]
                th = jnp.full_like(s1r, float("inf"))
                for a_ in reversed(range(k)):
                    th = jnp.where(s1r == v1[a_], theta[a_], th)
                f1 = jnp.exp(s1r - v1[0]) * inv_z
                f2 = jnp.exp(s2r - v2[0])
                for cc in range(ncol):
                    cs = slice(cc * LANES, (cc + 1) * LANES)
                    s2_ref[hd, cc, sl, :] = s2r[:, cs]
                    f2_ref[hd, cc, sl, :] = f2[:, cs]
                    th_ref[hd, cc, sl, :] = th[:, cs]
                    f1_ref[hd, cc, sl, :] = f1[:, cs]
            return 0

        lax.fori_loop(0, PEER_HEADS, per_head, 0)

    n_i1 = eb // N_KEYS
    for ii in range(n_i1):
        i1 = j * n_i1 + ii
        for cc in range(ncol):
            w = jnp.zeros((N_KEYS, LANES), F32)
            for hd in range(PEER_HEADS):
                th = th_ref[hd, cc, pl.ds(i1, 1), :]
                f1r = f1_ref[hd, cc, pl.ds(i1, 1), :]
                w = w + jnp.where(s2_ref[hd, cc] >= th, f2_ref[hd, cc], 0.0) * f1r
            w_ref[ii * N_KEYS:(ii + 1) * N_KEYS, cc * LANES:(cc + 1) * LANES] = w
    a = jnp.dot(u_ref[...], ht_ref[...], preferred_element_type=F32)
    act = 0.5 * a * (1.0 + lax.erf(a * (2.0 ** -0.5)))
    p = (w_ref[...] * act).astype(BF16)
    acc_ref[...] += lax.dot_general(p, v_ref[...], (((0,), (0,)), ((), ())),
                                    preferred_element_type=F32)

    @pl.when(j == nj - 1)
    def _():
        y = x_ref[...] + gt_ref[0] * acc_ref[...]
        if final_norm:
            r = lax.rsqrt(jnp.mean(y * y, axis=-1, keepdims=True) + EPS)
            y = (y * r) * og_ref[...]
        o_ref[...] = y


def peer_layer(x, g, sh, sc, gt, wq_t, k1, k2, u, v, out_g, tt, eb, final_norm=False):
    n = x.shape[0]
    nt = n // tt
    ncol = tt // LANES
    nj = N_EXPERTS // eb
    body = functools.partial(_peer_body, tt=tt, eb=eb, final_norm=final_norm)

    def grp(a):
        gg, r, c = a.shape
        tpg = nt // gg
        return pl.BlockSpec((1, r, c), lambda i, j: (i // tpg, 0, 0))

    return pl.pallas_call(
        body,
        grid=(nt, nj),
        in_specs=[
            pl.BlockSpec((tt, D_MODEL), lambda i, j: (i, 0)),
            pl.BlockSpec((1, D_MODEL), lambda i, j: (0, 0)),
            grp(sh), grp(sc), grp(gt),
            pl.BlockSpec(wq_t.shape, lambda i, j: (0, 0)),
            pl.BlockSpec(k1.shape, lambda i, j: (0, 0, 0)),
            pl.BlockSpec(k2.shape, lambda i, j: (0, 0, 0)),
            pl.BlockSpec((eb, D_MODEL), lambda i, j: (j, 0)),
            pl.BlockSpec((eb, D_MODEL), lambda i, j: (j, 0)),
            pl.BlockSpec((1, D_MODEL), lambda i, j: (0, 0)),
        ],
        out_specs=pl.BlockSpec((tt, D_MODEL), lambda i, j: (i, 0)),
        out_shape=jax.ShapeDtypeStruct((n, D_MODEL), F32),
        scratch_shapes=[
            pltpu.VMEM((D_MODEL, tt), BF16),
            pltpu.VMEM((PEER_HEADS, ncol, N_KEYS, LANES), F32),
            pltpu.VMEM((PEER_HEADS, ncol, N_KEYS, LANES), F32),
            pltpu.VMEM((PEER_HEADS, ncol, N_KEYS, LANES), F32),
            pltpu.VMEM((PEER_HEADS, ncol, N_KEYS, LANES), F32),
            pltpu.VMEM((eb, tt), F32),
            pltpu.VMEM((tt, D_MODEL), F32),
        ],
        compiler_params=_cparams(("arbitrary", "arbitrary")),
        name="peer",
    )(x, g, sh, sc, gt, wq_t, k1, k2, u, v, out_g)


def _row_groups(a, bsz, t, tm):
    if t % tm == 0:
        return a[:, None, :]
    assert tm % t == 0
    per_row = jnp.repeat(a, t, axis=0)
    return per_row.reshape(bsz * t // tm, tm, a.shape[1])


def _trunk(x, c_mod, start_pos, past, pool_buf, conv_buf, ssm_state, wts, tm, tt, eb):
    bsz, t, _ = x.shape
    n = bsz * t
    x2 = x.reshape(n, D_MODEL)
    rg = lambda a: _row_groups(a, bsz, t, tm)
    rgp = lambda a: _row_groups(a, bsz, t, tt)

    sh1, sc1, g1, sh2, sc2, g2 = c_mod[0]
    proj = matmul([x2], [wts["ab_w_in"]], tm, pro=(wts["norm_mix_g"][0], rg(sh1), rg(sc1)), name="ab_in")
    proj3 = proj.reshape(bsz, t, -1)
    k_new = proj3[..., D_A + D_B:D_A + 2 * D_B]
    v_new = proj3[..., D_A + 2 * D_B:]
    buf16 = jnp.pad(pool_buf, ((0, 0), (POOL_HALO - POOL_BUF, 0), (0, 0)))
    ya, nb16 = pool_mixer(proj3, buf16, wts["pool_wbd"], wts["pool_scale"], start_pos)
    new_pool = nb16[:, POOL_HALO - POOL_BUF:]
    if past is None:
        yb = attn_prompt(proj3, wts["sb_bias"])
    else:
        cache_k, cache_v, page_table = past
        page = cache_k.shape[2]
        q = proj3[..., D_A:D_A + D_B]
        padr = ((0, 0), (0, page - t), (0, 0))
        bias_rows = jnp.repeat(wts["sb_bias"], t)[:, None]
        yb = attn_sample(q, jnp.pad(k_new, padr), jnp.pad(v_new, padr), cache_k, cache_v, page_table, bias_rows)
    x2 = matmul([ya.reshape(n, D_A), yb.reshape(n, D_B)], [wts["ab_w_out_a"], wts["ab_w_out_b"]], tm,
                res=(x2, rg(g1)), name="ab_out")
    x2 = peer_layer(x2, wts["norm_ffn_g"][0], rgp(sh2), rgp(sc2), rgp(g2), wts["peer_wq_t"][0],
                    wts["peer_k1"][0], wts["peer_k2"][0], wts["peer_u"][0], wts["peer_v"][0],
                    wts["norm_out_g"], tt, eb)

    sh1, sc1, g1, sh2, sc2, g2 = c_mod[1]
    tm_wide = min(tm, 256)
    rgw = lambda a: _row_groups(a, bsz, t, tm_wide)
    proj = matmul([x2], [wts["ssd_w_in"]], tm_wide, pro=(wts["norm_mix_g"][1], rgw(sh1), rgw(sc1)),
                  name="ssd_in")
    proj3 = proj.reshape(bsz, t, -1)
    buf8 = jnp.pad(conv_buf, ((0, 0), (CONV_HALO - (SSD_CONV - 1), 0), (0, 0)))
    xact, nc8 = conv_silu(proj3, buf8, wts["ssd_conv_w"], wts["ssd_conv_b"])
    new_conv = nc8[:, CONV_HALO - (SSD_CONV - 1):]
    tp = -(-t // SSD_CHUNK) * SSD_CHUNK
    if tp != t:
        padt = ((0, 0), (0, tp - t), (0, 0))
        xact = jnp.pad(xact, padt)
        proj3 = jnp.pad(proj3, padt)
    h0 = ssm_state.reshape(bsz, SSD_HEADS // 2, LANES, SSD_STATE)
    ynorm, hlast = ssd_scan(xact, proj3, wts["ssd_dt_bias"], wts["ssd_a_log"], wts["ssd_d_full"],
                            wts["ssd_norm_g"], h0, t)
    new_ssm = hlast.reshape(bsz, SSD_HEADS, SSD_HEAD_DIM, SSD_STATE)
    ynorm = ynorm[:, :t].reshape(n, D_INNER)
    x2 = matmul([ynorm], [wts["ssd_w_out"]], tm, res=(x2, rg(g1)), name="ssd_out")
    x2 = peer_layer(x2, wts["norm_ffn_g"][1], rgp(sh2), rgp(sc2), rgp(g2), wts["peer_wq_t"][1],
                    wts["peer_k1"][1], wts["peer_k2"][1], wts["peer_u"][1], wts["peer_v"][1],
                    wts["norm_out_g"], tt, eb, final_norm=True)
    return (x2.reshape(bsz, t, D_MODEL), k_new.reshape(bsz, t, SB_HEADS, SB_HEAD_DIM)[None],
            v_new.reshape(bsz, t, SB_HEADS, SB_HEAD_DIM)[None], new_pool[None], new_conv[None], new_ssm[None])


def kernel(x_prompt, x_sample, cache_sb_k, cache_sb_v, state_pool, state_conv, state_ssm, page_table,
           c_prompt, c_sample, ada_w, ada_b, norm_mix_g, norm_ffn_g, norm_out_g,
           ab_w_in, ab_w_out, pool_w, pool_scale, sb_bias,
           ssd_w_in, ssd_conv_w, ssd_conv_b, ssd_dt_bias, ssd_a_log, ssd_d, ssd_norm_g, ssd_w_out,
           peer_wq, peer_k1, peer_k2, peer_u, peer_v):
    bsz, seq, _ = x_prompt.shape
    dbsz, dseq, _ = x_sample.shape
    depth = ada_w.shape[0]
    assert depth == 2 and ab_w_in.shape[0] == 1 and ssd_w_in.shape[0] == 1

    ssd_pad = (-D_SSD_IN) % LANES
    pad_lanes = lambda a: jnp.pad(a, ((0, 0), (0, LANES - a.shape[1])))
    wts = {
        "norm_mix_g": [norm_mix_g[l][None] for l in range(depth)],
        "norm_ffn_g": [norm_ffn_g[l][None] for l in range(depth)],
        "norm_out_g": norm_out_g[None],
        "ab_w_in": ab_w_in[0].astype(BF16),
        "ab_w_out_a": ab_w_out[0, :D_A].astype(BF16),
        "ab_w_out_b": ab_w_out[0, D_A:].astype(BF16),
        "pool_wbd": jax.scipy.linalg.block_diag(*[pool_w[0, g] for g in range(len(POOL_WINDOWS))]).astype(BF16),
        "pool_scale": pool_scale[0][None],
        "sb_bias": sb_bias[0],
        "ssd_w_in": jnp.pad(ssd_w_in[0], ((0, 0), (0, ssd_pad))).astype(BF16),
        "ssd_conv_w": ssd_conv_w[0],
        "ssd_conv_b": ssd_conv_b[0][None],
        "ssd_dt_bias": pad_lanes(ssd_dt_bias[0][None]),
        "ssd_a_log": pad_lanes(ssd_a_log[0][None]),
        "ssd_d_full": jnp.repeat(ssd_d[0], SSD_HEAD_DIM)[None],
        "ssd_norm_g": ssd_norm_g[0][None],
        "ssd_w_out": ssd_w_out[0].astype(BF16),
        "peer_wq_t": [peer_wq[l].T.astype(BF16) for l in range(depth)],
        "peer_k1": [peer_k1[l].astype(BF16) for l in range(depth)],
        "peer_k2": [peer_k2[l].astype(BF16) for l in range(depth)],
        "peer_u": [peer_u[l].astype(BF16) for l in range(depth)],
        "peer_v": [peer_v[l].astype(BF16) for l in range(depth)],
    }

    c_all = jnp.concatenate([c_prompt, c_sample], axis=0)
    mods_p, mods_s = [], []
    for l in range(depth):
        mod = matmul([c_all], [ada_w[l].astype(BF16)], c_all.shape[0], bias=ada_b[l][None], name="ada")
        parts = jnp.split(mod, 6, axis=-1)
        mods_p.append([p[:bsz] for p in parts])
        mods_s.append([p[bsz:] for p in parts])

    zeros = lambda shape: jnp.zeros(shape, F32)
    out_p = _trunk(x_prompt, mods_p, 0, None, zeros((bsz, POOL_BUF, D_A)), zeros((bsz, SSD_CONV - 1, CONV_DIM)),
                   zeros((bsz, SSD_HEADS, SSD_HEAD_DIM, SSD_STATE)), wts, tm=512, tt=512, eb=512)
    n_pool, page = cache_sb_k.shape[1], cache_sb_k.shape[2]
    past = (cache_sb_k.reshape(1, n_pool, page, D_B).astype(BF16),
            cache_sb_v.reshape(1, n_pool, page, D_B).astype(BF16), page_table)
    past_len = page_table.shape[1] * page
    out_s = _trunk(x_sample, mods_s, past_len, past, state_pool[0], state_conv[0], state_ssm[0], wts,
                   tm=dbsz * dseq, tt=dbsz * dseq, eb=512)
    y_p, k_p, v_p, pool_p, conv_p, ssm_p = out_p
    y_s, k_s, v_s, pool_s, conv_s, ssm_s = out_s
    return (y_p, y_s, k_p, v_p, k_s, v_s, pool_p, pool_s, conv_p, conv_s, ssm_p, ssm_s)
```
